```python
import math
import jax, jax.numpy as jnp
from jax import lax
import numpy as np

D_MODEL = 1024
BATCH = 8
SEQ = 4096
DEPTH = 2
DEC_BATCH = 8
DEC_SEQ = 64
PAST_LEN = 1024

CHUNK = 64
N_META = 16
SCAN_BLOCK = CHUNK
MIX_WIDTH = D_MODEL
SSM_WIDTH = MIX_WIDTH // 2
SSM_HEADDIM = 64
SSM_HEADS = SSM_WIDTH // SSM_HEADDIM
SSM_GROUPS = 2
SSM_STATE = 128
CONV_WIDTH = 4
CONV_CH = SSM_WIDTH + 2 * SSM_GROUPS * SSM_STATE
HG_WIDTH = MIX_WIDTH - SSM_WIDTH
HG_EXPAND = 128
HG_HEADS = HG_WIDTH // HG_EXPAND
HG_VDIM = HG_WIDTH // HG_HEADS
IN_SPLITS = tuple(np.cumsum([SSM_WIDTH, CONV_CH, SSM_HEADS, HG_WIDTH, HG_WIDTH, HG_WIDTH]).tolist())
IN_COLS = SSM_WIDTH + CONV_CH + SSM_HEADS + 4 * HG_WIDTH
D_FF = 2816
EPS = 1e-6
LB_FLOOR = 1e-30

kernel_name = "hymba_ssd_hgrn2_macaron_stream"


def _rmsnorm(x, w, groups=1):
    xf = x.astype(jnp.float32)
    shp = xf.shape
    xg = xf.reshape(shp[:-1] + (groups, shp[-1] // groups))
    xg = xg * lax.rsqrt(jnp.mean(xg * xg, axis=-1, keepdims=True) + EPS)
    return (xg.reshape(shp) * w.astype(jnp.float32)).astype(x.dtype)


def _swiglu(x, w_gate, w_up, w_down):
    return (jax.nn.silu(x @ w_gate) * (x @ w_up)) @ w_down


def _causal_conv(u, buf, w, b):
    L = u.shape[1]
    xp = jnp.concatenate([buf.astype(u.dtype), u], axis=1)
    out = b + xp[:, 0:L] * w[0]
    for k in range(1, CONV_WIDTH):
        out = out + xp[:, k:k + L] * w[k]
    return out, xp[:, xp.shape[1] - (CONV_WIDTH - 1):]


def _pad(a, Lp):
    return jnp.pad(a, [(0, 0), (0, Lp - a.shape[1])] + [(0, 0)] * (a.ndim - 2))


def _to_blocks(a, nb):
    bt = a.shape[0]
    return jnp.moveaxis(a.reshape((bt, nb, SCAN_BLOCK) + a.shape[2:]), 1, 0)


def _masked_decay(seg, causal):
    return jnp.where(causal, jnp.exp(jnp.where(causal, seg, 0.0)), 0.0)


def _ssd_scan(xdt, la, bm, cm, s0):
    bt, L, H, P = xdt.shape
    G, N = bm.shape[2], bm.shape[3]
    R = H // G
    nb = L // SCAN_BLOCK
    causal = jnp.tril(jnp.ones((SCAN_BLOCK, SCAN_BLOCK), bool))[None, :, :, None, None]
    xs = (_to_blocks(xdt.reshape(bt, L, G, R, P), nb), _to_blocks(la.reshape(bt, L, G, R), nb),
          _to_blocks(bm, nb), _to_blocks(cm, nb))

    def step(S, blk):
        xb, lab, bb, cb = blk
        cum = jnp.cumsum(lab, axis=1)
        seg = cum[:, :, None] - cum[:, None]
        decay = _masked_decay(seg, causal)
        cbm = jnp.einsum('bign,bjgn->bijg', cb, bb)
        y = jnp.einsum('bijg,bijgr,bjgrp->bigrp', cbm, decay, xb)
        y = y + jnp.einsum('bign,bgrpn,bigr->bigrp', cb, S, jnp.exp(cum))
        last = cum[:, -1]
        S = jnp.exp(last)[..., None, None] * S + jnp.einsum(
            'bjgn,bjgr,bjgrp->bgrpn', bb, jnp.exp(last[:, None] - cum), xb)
        return S, y

    S, ys = lax.scan(step, s0.reshape(bt, G, R, P, N), xs)
    y = jnp.moveaxis(ys, 0, 1).reshape(bt, L, H, P)
    return y, S.reshape(bt, H, P, N)


def _gla_scan(q, k, v, lf, s0):
    bt, L, H, K = q.shape
    nb = L // SCAN_BLOCK
    causal = jnp.tril(jnp.ones((SCAN_BLOCK, SCAN_BLOCK), bool))[None, :, :, None, None]
    xs = (_to_blocks(q, nb), _to_blocks(k, nb), _to_blocks(v, nb), _to_blocks(lf, nb))

    def step(S, blk):
        qb, kb, vb, fb = blk
        cum = jnp.cumsum(fb, axis=1)
        seg = cum[:, :, None] - cum[:, None]
        decay = _masked_decay(seg, causal)
        att = jnp.einsum('bihk,bjhk,bijhk->bijh', qb, kb, decay)
        o = jnp.einsum('bijh,bjhv->bihv', att, vb) + jnp.einsum('bihk,bhkv->bihv', qb * jnp.exp(cum), S)
        last = cum[:, -1]
        S = jnp.exp(last)[..., None] * S + jnp.einsum(
            'bjhk,bjhv->bhkv', kb * jnp.exp(last[:, None] - cum), vb)
        return S, o

    S, os_ = lax.scan(step, s0, xs)
    return jnp.moveaxis(os_, 0, 1).reshape(bt, L, H, v.shape[-1]), S


def _mixer(hn, conv_buf, ssm_s, hg_s, w_in, conv_w, conv_b, dt_bias, a_log, d_skip,
           ssm_norm_w, hg_lb, hg_norm_w, w_out):
    f32 = jnp.float32
    bt, L, _ = hn.shape
    Lp = -(-L // SCAN_BLOCK) * SCAN_BLOCK
    proj = hn @ w_in
    z, xbc, dt_raw, hq, hf, hi, hg = jnp.split(proj, IN_SPLITS, axis=-1)

    xbc, new_conv = _causal_conv(xbc, conv_buf, conv_w, conv_b)
    xbc = jax.nn.silu(xbc).astype(f32)
    xs, bm, cm = jnp.split(xbc, [SSM_WIDTH, SSM_WIDTH + SSM_GROUPS * SSM_STATE], axis=-1)
    xs = xs.reshape(bt, L, SSM_HEADS, SSM_HEADDIM)
    bm = bm.reshape(bt, L, SSM_GROUPS, SSM_STATE)
    cm = cm.reshape(bt, L, SSM_GROUPS, SSM_STATE)
    dt = jax.nn.softplus(dt_raw.astype(f32) + dt_bias.astype(f32))
    a = -jnp.exp(a_log.astype(f32))
    y, new_ssm = _ssd_scan(_pad(xs * dt[..., None], Lp), _pad(dt * a, Lp), _pad(bm, Lp), _pad(cm, Lp),
                           ssm_s.astype(f32))
    y = y[:, :L] + d_skip.astype(f32)[:, None] * xs
    y = y.reshape(bt, L, SSM_WIDTH) * jax.nn.silu(z.astype(f32))
    y_ssm = _rmsnorm(y, ssm_norm_w, SSM_GROUPS)

    q = jax.nn.silu(hq.astype(f32)).reshape(bt, L, HG_HEADS, HG_EXPAND)
    fr = hf.astype(f32)
    lf = jnp.logaddexp(jnp.log(jnp.maximum(hg_lb, LB_FLOOR)), jnp.log1p(-hg_lb) + jax.nn.log_sigmoid(fr))
    kk = (1.0 - hg_lb) * jax.nn.sigmoid(-fr)
    v = hi.astype(f32).reshape(bt, L, HG_HEADS, HG_VDIM)
    o, new_hg = _gla_scan(_pad(q, Lp), _pad(kk.reshape(bt, L, HG_HEADS, HG_EXPAND), Lp), _pad(v, Lp),
                          _pad(lf.reshape(bt, L, HG_HEADS, HG_EXPAND), Lp), hg_s.astype(f32))
    o = _rmsnorm(o[:, :L], hg_norm_w.reshape(HG_HEADS, HG_VDIM))
    o = o.reshape(bt, L, HG_WIDTH) * jax.nn.silu(hg.astype(f32))

    mixed = jnp.concatenate([y_ssm, o], axis=-1).astype(hn.dtype) @ w_out
    return (mixed, new_conv.astype(conv_buf.dtype), new_ssm.astype(ssm_s.dtype), new_hg.astype(hg_s.dtype))


def _trunk(h, conv_st, ssm_st, hg_st, p):
    (ln_ffa_w, ffa_w_gate, ffa_w_up, ffa_w_down, ln_mix_w, w_in, conv_w, conv_b, dt_bias, a_log, d_skip,
     ssm_norm_w, hg_lb, hg_norm_w, w_out, ln_ffb_w, ffb_w_gate, ffb_w_up, ffb_w_down, ln_f_w) = p
    conv_new, ssm_new, hg_new = [], [], []
    for l in range(DEPTH):
        h = h + 0.5 * _swiglu(_rmsnorm(h, ln_ffa_w[l]), ffa_w_gate[l], ffa_w_up[l], ffa_w_down[l])
        m, c, s, g = _mixer(_rmsnorm(h, ln_mix_w[l]), conv_st[l], ssm_st[l], hg_st[l], w_in[l], conv_w[l],
                            conv_b[l], dt_bias[l], a_log[l], d_skip[l], ssm_norm_w[l], hg_lb[l], hg_norm_w[l],
                            w_out[l])
        h = h + m
        h = h + 0.5 * _swiglu(_rmsnorm(h, ln_ffb_w[l]), ffb_w_gate[l], ffb_w_up[l], ffb_w_down[l])
        conv_new.append(c)
        ssm_new.append(s)
        hg_new.append(g)
    return _rmsnorm(h, ln_f_w), jnp.stack(conv_new), jnp.stack(ssm_new), jnp.stack(hg_new)


def setup_inputs(seed: int = 0) -> dict:
    key = jax.random.key(seed)
    ks = jax.random.split(key, 32)
    f32 = jnp.float32

    def nrm(k, shape, s):
        return s * jax.random.normal(k, shape, f32)

    dt0 = jnp.exp(jax.random.uniform(ks[12], (DEPTH, SSM_HEADS), f32, math.log(1e-3), math.log(1e-1)))
    return {
        "x_prompt": nrm(ks[0], (BATCH, SEQ, D_MODEL), 1.0),
        "x_sample": nrm(ks[1], (DEC_BATCH, DEC_SEQ, D_MODEL), 1.0),
        "state_conv": nrm(ks[2], (DEPTH, DEC_BATCH, CONV_WIDTH - 1, CONV_CH), 1.0),
        "state_ssm": nrm(ks[3], (DEPTH, DEC_BATCH, SSM_HEADS, SSM_HEADDIM, SSM_STATE), 0.5),
        "state_hgrn": nrm(ks[4], (DEPTH, DEC_BATCH, HG_HEADS, HG_EXPAND, HG_VDIM), 0.5),
        "meta_tokens": nrm(ks[5], (N_META, D_MODEL), 1.0),
        "ln_ffa_w": 1.0 + nrm(ks[6], (DEPTH, D_MODEL), 0.01),
        "ffa_w_gate": nrm(ks[7], (DEPTH, D_MODEL, D_FF), D_MODEL ** -0.5),
        "ffa_w_up": nrm(ks[8], (DEPTH, D_MODEL, D_FF), D_MODEL ** -0.5),
        "ffa_w_down": nrm(ks[9], (DEPTH, D_FF, D_MODEL), D_FF ** -0.5),
        "ln_mix_w": 1.0 + nrm(ks[10], (DEPTH, D_MODEL), 0.01),
        "w_in": nrm(ks[11], (DEPTH, D_MODEL, IN_COLS), D_MODEL ** -0.5),
        "conv_w": nrm(ks[13], (DEPTH, CONV_WIDTH, CONV_CH), CONV_WIDTH ** -0.5),
        "conv_b": nrm(ks[14], (DEPTH, CONV_CH), 0.01),
        "dt_bias": dt0 + jnp.log(-jnp.expm1(-dt0)),
        "a_log": jnp.log(jax.random.uniform(ks[15], (DEPTH, SSM_HEADS), f32, 1.0, 16.0)),
        "d_skip": 1.0 + nrm(ks[16], (DEPTH, SSM_HEADS), 0.01),
        "ssm_norm_w": 1.0 + nrm(ks[17], (DEPTH, SSM_WIDTH), 0.01),
        "hg_lb_raw": nrm(ks[18], (DEPTH, HG_WIDTH), 0.1),
        "hg_norm_w": 1.0 + nrm(ks[19], (DEPTH, HG_WIDTH), 0.01),
        "w_out": nrm(ks[20], (DEPTH, MIX_WIDTH, D_MODEL), MIX_WIDTH ** -0.5),
        "ln_ffb_w": 1.0 + nrm(ks[21], (DEPTH, D_MODEL), 0.01),
        "ffb_w_gate": nrm(ks[22], (DEPTH, D_MODEL, D_FF), D_MODEL ** -0.5),
        "ffb_w_up": nrm(ks[23], (DEPTH, D_MODEL, D_FF), D_MODEL ** -0.5),
        "ffb_w_down": nrm(ks[24], (DEPTH, D_FF, D_MODEL), D_FF ** -0.5),
        "ln_f_w": 1.0 + nrm(ks[25], (D_MODEL,), 0.01),
    }


def reference(x_prompt, x_sample, state_conv, state_ssm, state_hgrn, meta_tokens, ln_ffa_w, ffa_w_gate,
              ffa_w_up, ffa_w_down, ln_mix_w, w_in, conv_w, conv_b, dt_bias, a_log, d_skip, ssm_norm_w,
              hg_lb_raw, hg_norm_w, w_out, ln_ffb_w, ffb_w_gate, ffb_w_up, ffb_w_down, ln_f_w):
    sm = jax.nn.softmax(hg_lb_raw.astype(jnp.float32), axis=0)
    hg_lb = jnp.clip(jnp.cumsum(sm, axis=0) - sm[0], 0.0, 1.0 - 1e-6)
    params = (ln_ffa_w, ffa_w_gate, ffa_w_up, ffa_w_down, ln_mix_w, w_in, conv_w, conv_b, dt_bias, a_log,
              d_skip, ssm_norm_w, hg_lb, hg_norm_w, w_out, ln_ffb_w, ffb_w_gate, ffb_w_up, ffb_w_down, ln_f_w)

    b = x_prompt.shape[0]
    dt_ = x_prompt.dtype
    meta = jnp.broadcast_to(meta_tokens.astype(dt_)[None], (b, N_META, D_MODEL))
    h0 = jnp.concatenate([meta, x_prompt], axis=1)
    zc = jnp.zeros((DEPTH, b, CONV_WIDTH - 1, CONV_CH), dt_)
    zs = jnp.zeros((DEPTH, b, SSM_HEADS, SSM_HEADDIM, SSM_STATE), dt_)
    zg = jnp.zeros((DEPTH, b, HG_HEADS, HG_EXPAND, HG_VDIM), dt_)
    hp, conv_p, ssm_p, hg_p = _trunk(h0, zc, zs, zg, params)
    y_prompt = hp[:, N_META:]

    y_sample, conv_s, ssm_s, hg_s = _trunk(x_sample, state_conv, state_ssm, state_hgrn, params)
    return (y_prompt, y_sample, conv_p, ssm_p, hg_p, conv_s, ssm_s, hg_s)
```

```python
import functools

import numpy as np
import jax
import jax.numpy as jnp
from jax import lax
from jax.experimental import pallas as pl
from jax.experimental.pallas import tpu as pltpu

F32 = jnp.float32
BF16 = jnp.bfloat16

D_MODEL = 1024
N_STREAMS = 8
BLK = 64
ROWS = N_STREAMS * BLK
N_META = 16
N_PAD = BLK - N_META
SSM_WIDTH = 512
SSM_GROUPS = 2
SSM_GROUP_WIDTH = SSM_WIDTH // SSM_GROUPS
SSM_HEADDIM = 64
SSM_STATE = 128
CONV_CH = 1024
CONV_WIDTH = 4
HG_WIDTH = 512
HG_HEADS = 4
HG_DIM = 128
D_FF = 2816
FF_CHUNK = 1408
EPS = 1e-6
LB_FLOOR = 1e-30
SUB = 8

C_Z, C_XBC, C_Q, C_F, C_I, C_G = 0, 512, 1536, 2048, 2560, 3072
IN_MAIN = 3584

VMEM_LIMIT_BYTES = 56 * 1024 * 1024


def _rms(x, w):
    return x * lax.rsqrt(jnp.mean(x * x, axis=-1, keepdims=True) + EPS) * w


def _silu(x):
    return x / (1.0 + jnp.exp(-x))


def _softplus(x):
    return jnp.maximum(x, 0.0) + jnp.log1p(jnp.exp(-jnp.abs(x)))


def _dot(a, b):
    return jnp.dot(a, b, preferred_element_type=F32)


def _dot_nt(a, b):
    return lax.dot_general(a, b, (((1,), (1,)), ((), ())), preferred_element_type=F32)


def _dot_tn(a, b):
    return lax.dot_general(a, b, (((0,), (0,)), ((), ())), preferred_element_type=F32)


def _cumsum_rows(x, tri):
    return jnp.dot(tri, x, preferred_element_type=F32, precision=lax.Precision.HIGHEST)


def _ffn_kernel(h_ref, lnw_ref, wg_ref, wu_ref, wd_ref, lnf_ref, o_ref, *, final_norm):
    h = h_ref[...]
    hn = _rms(h, lnw_ref[...]).astype(BF16)
    acc = jnp.zeros_like(h)
    for c in range(D_FF // FF_CHUNK):
        cs = slice(c * FF_CHUNK, (c + 1) * FF_CHUNK)
        g = _dot(hn, wg_ref[:, cs])
        u = _dot(hn, wu_ref[:, cs])
        a = (_silu(g) * u).astype(BF16)
        acc = acc + _dot(a, wd_ref[cs, :])
    out = h + 0.5 * acc
    if final_norm:
        out = _rms(out, lnf_ref[...])
    o_ref[...] = out


def _const_spec(shape):
    nd = len(shape)
    return pl.BlockSpec(shape, lambda i: (0,) * nd, pipeline_mode=pl.Buffered(1))


def _ffn(h, lnw, wg, wu, wd, lnf, final_norm):
    n_rows = h.shape[0]
    row_spec = pl.BlockSpec((ROWS, D_MODEL), lambda i: (i, 0))
    return pl.pallas_call(
        functools.partial(_ffn_kernel, final_norm=final_norm),
        grid=(n_rows // ROWS,),
        in_specs=[row_spec, _const_spec((1, D_MODEL)), _const_spec((D_MODEL, D_FF)),
                  _const_spec((D_MODEL, D_FF)), _const_spec((D_FF, D_MODEL)),
                  _const_spec((1, D_MODEL))],
        out_specs=row_spec,
        out_shape=jax.ShapeDtypeStruct(h.shape, F32),
        compiler_params=pltpu.CompilerParams(
            dimension_semantics=("parallel",), vmem_limit_bytes=VMEM_LIMIT_BYTES),
        name="ffn_final" if final_norm else "ffn",
    )(h, lnw, wg, wu, wd, lnf)


def _mixer_kernel(h_ref, lnw_ref, win_ref, wdt_ref, convw_ref, convb_ref, dtb_ref, alog_ref,
                  dskip_ref, ssmnw_ref, lbraw_ref, hgnw_ref, wout_ref, emat_ref,
                  sconv_ref, sssm_ref, shg_ref,
                  hout_ref, oconv_ref, ossm_ref, ohg_ref,
                  proj_ref, dtraw_ref, mix_ref, pbuf_ref, *, layer, n_prompt_blocks):
    step = pl.program_id(0)

    @pl.when(step == 0)
    def _():
        oconv_ref[...] = jnp.zeros(oconv_ref.shape, F32)
        ossm_ref[...] = jnp.zeros(ossm_ref.shape, F32)
        ohg_ref[...] = jnp.zeros(ohg_ref.shape, F32)

    @pl.when(step == n_prompt_blocks)
    def _():
        oconv_ref[0] = sconv_ref[...]
        ossm_ref[0] = sssm_ref[...]
        ohg_ref[0] = shg_ref[...]

    hn = _rms(h_ref[...], lnw_ref[...]).astype(BF16)
    proj_ref[...] = _dot(hn, win_ref[...])
    dtraw_ref[...] = _dot(hn, wdt_ref[...])

    first_valid = jnp.where(step == 0, N_PAD, 0)
    t_col = lax.broadcasted_iota(jnp.int32, (BLK, 1), 0)
    valid = t_col >= first_valid

    tri = (lax.broadcasted_iota(jnp.int32, (BLK, BLK), 0)
           >= lax.broadcasted_iota(jnp.int32, (BLK, BLK), 1)).astype(F32)
    sub_iota = lax.broadcasted_iota(jnp.int32, (SUB, 1), 0)

    conv_w = convw_ref[...]
    conv_b = convb_ref[...]
    a_neg = -jnp.exp(alog_ref[...])
    row_w = lax.broadcasted_iota(jnp.int32, (BLK, SSM_WIDTH), 0)
    col_w = lax.broadcasted_iota(jnp.int32, (BLK, SSM_WIDTH), 1)
    diag_pick = (col_w % BLK) == row_w
    row_g = lax.broadcasted_iota(jnp.int32, (BLK, SSM_GROUP_WIDTH), 0)
    col_g = lax.broadcasted_iota(jnp.int32, (BLK, SSM_GROUP_WIDTH), 1)
    causal_g = (col_g % BLK) <= row_g
    bd_r = lax.broadcasted_iota(jnp.int32, (SSM_GROUP_WIDTH, SSM_GROUP_WIDTH), 0)
    bd_c = lax.broadcasted_iota(jnp.int32, (SSM_GROUP_WIDTH, SSM_GROUP_WIDTH), 1)
    head_diag = (bd_r // BLK) == (bd_c // BLK)

    raw = lbraw_ref[...]
    e_raw = jnp.exp(raw - jnp.max(raw, axis=0, keepdims=True))
    sm = e_raw / jnp.sum(e_raw, axis=0, keepdims=True)
    lb = jnp.clip(jnp.sum(sm[0:layer + 1], axis=0, keepdims=True) - sm[0:1], 0.0, 1.0 - 1e-6)
    log_lb = jnp.log(jnp.maximum(lb, LB_FLOOR))
    log_1m_lb = jnp.log1p(-lb)
    one_m_lb = 1.0 - lb
    ai = lax.broadcasted_iota(jnp.int32, (BLK, BLK), 0)
    aj = lax.broadcasted_iota(jnp.int32, (BLK, BLK), 1)
    level_masks = {s: (ai // (2 * s)) == (aj // (2 * s)) for s in (32, 16, 8)}
    sub_mask = (ai // SUB) == (aj // SUB)

    def per_stream(b, carry):
        r0 = pl.multiple_of(b * BLK, BLK)
        rows = pl.ds(r0, BLK)

        x = proj_ref[rows, C_XBC:C_XBC + CONV_CH]
        tail = oconv_ref[0, b]
        oconv_ref[0, b] = x[BLK - SUB:BLK]
        conv = conv_b + x * conv_w[CONV_WIDTH - 1:CONV_WIDTH]
        for s in range(1, CONV_WIDTH):
            rolled = pltpu.roll(x, s, 0)
            head = jnp.where(sub_iota < s, pltpu.roll(tail, s, 0), rolled[0:SUB])
            shifted = jnp.concatenate([head, rolled[SUB:]], axis=0)
            conv = conv + shifted * conv_w[CONV_WIDTH - 1 - s:CONV_WIDTH - s]
        xbc = _silu(conv)
        xs = xbc[:, 0:SSM_WIDTH]

        dt = _softplus(dtraw_ref[rows, :] + dtb_ref[...])
        dt = jnp.where(valid, dt, 0.0)
        cum = _cumsum_rows(dt * a_neg, tri)
        cum_row = jnp.sum(jnp.where(diag_pick, cum, 0.0), axis=0, keepdims=True)
        y_parts = []
        for g in range(SSM_GROUPS):
            cs = slice(g * SSM_GROUP_WIDTH, (g + 1) * SSM_GROUP_WIDTH)
            b_g = xbc[:, SSM_WIDTH + g * SSM_STATE:SSM_WIDTH + (g + 1) * SSM_STATE].astype(BF16)
            c_g = xbc[:, SSM_WIDTH + (SSM_GROUPS + g) * SSM_STATE:
                      SSM_WIDTH + (SSM_GROUPS + g + 1) * SSM_STATE].astype(BF16)
            cum_g = cum[:, cs]
            cbm = _dot_nt(c_g, jnp.concatenate([b_g] * 4, axis=0))
            seg = cum_g - cum_row[:, cs]
            decay = jnp.where(causal_g, jnp.exp(jnp.where(causal_g, seg, 0.0)), 0.0)
            xs_g = xs[:, cs]
            xdt = xs_g * dt[:, cs]
            xdt_bd = jnp.where(head_diag, jnp.concatenate([xdt] * 4, axis=0), 0.0).astype(BF16)
            st = ossm_ref[0, b, g]
            y_g = _dot((cbm * decay).astype(BF16), xdt_bd)
            y_g = y_g + _dot(c_g, st.astype(BF16)) * jnp.exp(cum_g)
            y_g = y_g + dskip_ref[:, cs] * xs_g
            last = cum_g[BLK - 1:BLK]
            xdt_w = (xdt * jnp.exp(last - cum_g)).astype(BF16)
            ossm_ref[0, b, g] = jnp.exp(last) * st + _dot_tn(b_g, xdt_w)
            y_g = y_g * _silu(proj_ref[rows, C_Z + g * SSM_GROUP_WIDTH:C_Z + (g + 1) * SSM_GROUP_WIDTH])
            y_g = y_g * lax.rsqrt(jnp.mean(y_g * y_g, axis=-1, keepdims=True) + EPS)
            y_parts.append(y_g * ssmnw_ref[:, cs])
        mix_ref[rows, 0:SSM_WIDTH] = jnp.concatenate(y_parts, axis=1).astype(BF16)

        q = _silu(proj_ref[rows, C_Q:C_Q + HG_WIDTH])
        fr = proj_ref[rows, C_F:C_F + HG_WIDTH]
        log_sig = jnp.minimum(fr, 0.0) - jnp.log1p(jnp.exp(-jnp.abs(fr)))
        t2 = log_1m_lb + log_sig
        lf = jnp.maximum(log_lb, t2) + jnp.log1p(jnp.exp(-jnp.abs(log_lb - t2)))
        kk = one_m_lb / (1.0 + jnp.exp(fr))
        lf = jnp.where(valid, lf, 0.0)
        kk = jnp.where(valid, kk, 0.0)
        v = proj_ref[rows, C_I:C_I + HG_WIDTH]
        cumh = _cumsum_rows(lf, tri)
        lasth = cumh[BLK - 1:BLK]
        q_in = (q * jnp.exp(cumh)).astype(BF16)
        k_out = (kk * jnp.exp(lasth - cumh)).astype(BF16)
        v16 = v.astype(BF16)

        att = [jnp.zeros((BLK, BLK), F32) for _ in range(HG_HEADS)]
        for s in (32, 16, 8):
            upper = ((t_col // s) % 2) == 1
            ref_rows = jnp.concatenate(
                [jnp.broadcast_to(cumh[m + s - 1:m + s], (2 * s, HG_WIDTH)) for m in range(0, BLK, 2 * s)],
                axis=0)
            q_s = jnp.where(upper, q * jnp.exp(cumh - ref_rows), 0.0).astype(BF16)
            k_s = jnp.where(upper, 0.0, kk * jnp.exp(ref_rows - cumh)).astype(BF16)
            for hh in range(HG_HEADS):
                hs = slice(hh * HG_DIM, (hh + 1) * HG_DIM)
                att[hh] = att[hh] + jnp.where(level_masks[s], _dot_nt(q_s[:, hs], k_s[:, hs]), 0.0)
        for sc in range(BLK // SUB):
            rs = slice(sc * SUB, (sc + 1) * SUB)
            q_c, k_c, c_c = q[rs], kk[rs], cumh[rs]
            for j in range(SUB):
                pair = jnp.where(sub_iota >= j, q_c * jnp.exp(c_c - c_c[j:j + 1]) * k_c[j:j + 1], 0.0)
                pbuf_ref[rs, j * HG_WIDTH:(j + 1) * HG_WIDTH] = pair
        att_d = _dot(pbuf_ref[...].astype(BF16), emat_ref[...])
        o_parts = []
        for hh in range(HG_HEADS):
            hs = slice(hh * HG_DIM, (hh + 1) * HG_DIM)
            a_h = att[hh] + jnp.where(sub_mask, att_d[:, hh * BLK:(hh + 1) * BLK], 0.0)
            st = ohg_ref[0, b, hh]
            o_h = _dot(a_h.astype(BF16), v16[:, hs]) + _dot_nt(q_in[:, hs], st.astype(BF16))
            ohg_ref[0, b, hh] = st * jnp.exp(lasth[:, hs]) + _dot_tn(v16[:, hs], k_out[:, hs])
            o_parts.append(o_h * lax.rsqrt(jnp.mean(o_h * o_h, axis=-1, keepdims=True) + EPS))
        o = jnp.concatenate(o_parts, axis=1) * hgnw_ref[...]
        o = o * _silu(proj_ref[rows, C_G:C_G + HG_WIDTH])
        mix_ref[rows, SSM_WIDTH:SSM_WIDTH + HG_WIDTH] = o.astype(BF16)
        return carry

    lax.fori_loop(0, N_STREAMS, per_stream, 0)
    hout_ref[...] = h_ref[...] + _dot(mix_ref[...], wout_ref[...])


def _diag_reduce_matrix():
    r = np.arange(SUB * HG_WIDTH)
    c = np.arange(HG_HEADS * BLK)
    j, head_r = r // HG_WIDTH, (r % HG_WIDTH) // HG_DIM
    head_c, cc = c // BLK, c % BLK
    m = (head_r[:, None] == head_c[None, :]) & (j[:, None] == (cc % SUB)[None, :])
    return jnp.asarray(m, dtype=BF16)


def _mixer(h, p, layer, n_prompt_blocks, sconv, sssm, shg):
    n_steps = h.shape[0] // ROWS
    row_spec = pl.BlockSpec((ROWS, D_MODEL), lambda i: (i, 0))
    group = lambda i: i // n_prompt_blocks
    conv_shape = (2, N_STREAMS, SUB, CONV_CH)
    ssm_shape = (2, N_STREAMS, SSM_GROUPS, SSM_STATE, SSM_GROUP_WIDTH)
    hg_shape = (2, N_STREAMS, HG_HEADS, HG_DIM, HG_DIM)
    in_specs = [
        row_spec,
        _const_spec((1, D_MODEL)),
        _const_spec((D_MODEL, IN_MAIN)),
        _const_spec((D_MODEL, SSM_WIDTH)),
        _const_spec((CONV_WIDTH, CONV_CH)),
        _const_spec((1, CONV_CH)),
        _const_spec((1, SSM_WIDTH)),
        _const_spec((1, SSM_WIDTH)),
        _const_spec((1, SSM_WIDTH)),
        _const_spec((1, SSM_WIDTH)),
        _const_spec((p["lb_raw"].shape[0], HG_WIDTH)),
        _const_spec((1, HG_WIDTH)),
        _const_spec((D_MODEL, D_MODEL)),
        _const_spec((SUB * HG_WIDTH, HG_HEADS * BLK)),
        _const_spec(conv_shape[1:]),
        _const_spec(ssm_shape[1:]),
        _const_spec(hg_shape[1:]),
    ]
    out_specs = [
        row_spec,
        pl.BlockSpec((1,) + conv_shape[1:], lambda i: (group(i), 0, 0, 0)),
        pl.BlockSpec((1,) + ssm_shape[1:], lambda i: (group(i), 0, 0, 0, 0)),
        pl.BlockSpec((1,) + hg_shape[1:], lambda i: (group(i), 0, 0, 0, 0)),
    ]
    out_shape = [
        jax.ShapeDtypeStruct(h.shape, F32),
        jax.ShapeDtypeStruct(conv_shape, F32),
        jax.ShapeDtypeStruct(ssm_shape, F32),
        jax.ShapeDtypeStruct(hg_shape, F32),
    ]
    return pl.pallas_call(
        functools.partial(_mixer_kernel, layer=layer, n_prompt_blocks=n_prompt_blocks),
        grid=(n_steps,),
        in_specs=in_specs,
        out_specs=out_specs,
        out_shape=out_shape,
        scratch_shapes=[
            pltpu.VMEM((ROWS, IN_MAIN), F32),
            pltpu.VMEM((ROWS, SSM_WIDTH), F32),
            pltpu.VMEM((ROWS, D_MODEL), BF16),
            pltpu.VMEM((BLK, SUB * HG_WIDTH), F32),
        ],
        compiler_params=pltpu.CompilerParams(
            dimension_semantics=("arbitrary",), vmem_limit_bytes=VMEM_LIMIT_BYTES),
        name="mixer",
    )(h, p["ln_mix"], p["w_in"], p["w_dt"], p["conv_w"], p["conv_b"], p["dt_bias"], p["a_log"],
      p["d_skip"], p["ssm_norm"], p["lb_raw"], p["hg_norm"], p["w_out"], _diag_reduce_matrix(),
      sconv, sssm, shg)


def _expand_heads(x):
    return jnp.repeat(x.astype(F32), SSM_HEADDIM)[None, :]


def kernel(x_prompt, x_sample, state_conv, state_ssm, state_hgrn, meta_tokens, ln_ffa_w, ffa_w_gate, ffa_w_up, ffa_w_down, ln_mix_w, w_in, conv_w, conv_b, dt_bias, a_log, d_skip, ssm_norm_w, hg_lb_raw, hg_norm_w, w_out, ln_ffb_w, ffb_w_gate, ffb_w_up, ffb_w_down, ln_f_w):
    depth = w_in.shape[0]
    nb, seq, _ = x_prompt.shape
    assert nb == N_STREAMS and x_sample.shape[:2] == (N_STREAMS, BLK) and seq % BLK == 0
    n_seq_blocks = seq // BLK
    n_prompt_blocks = n_seq_blocks + 1

    first = jnp.concatenate(
        [jnp.zeros((nb, N_PAD, D_MODEL), F32),
         jnp.broadcast_to(meta_tokens.astype(F32)[None], (nb, N_META, D_MODEL))], axis=1)
    xp = x_prompt.astype(F32).reshape(nb, n_seq_blocks, BLK, D_MODEL).transpose(1, 0, 2, 3)
    h = jnp.concatenate([first[None], xp, x_sample.astype(F32)[None]], axis=0)
    h = h.reshape((n_prompt_blocks + 1) * ROWS, D_MODEL)

    row = lambda a: a.astype(F32)[None, :]
    dt_cols = slice(SSM_WIDTH + CONV_CH, SSM_WIDTH + CONV_CH + SSM_WIDTH // SSM_HEADDIM)
    conv_out, ssm_out, hg_out = [], [], []
    for l in range(depth):
        w = w_in[l]
        params = {
            "ln_mix": row(ln_mix_w[l]),
            "w_in": jnp.concatenate([w[:, :dt_cols.start], w[:, dt_cols.stop:]], axis=1).astype(BF16),
            "w_dt": jnp.repeat(w[:, dt_cols], SSM_HEADDIM, axis=1).astype(BF16),
            "conv_w": conv_w[l].astype(F32),
            "conv_b": row(conv_b[l]),
            "dt_bias": _expand_heads(dt_bias[l]),
            "a_log": _expand_heads(a_log[l]),
            "d_skip": _expand_heads(d_skip[l]),
            "ssm_norm": row(ssm_norm_w[l]),
            "lb_raw": hg_lb_raw.astype(F32),
            "hg_norm": row(hg_norm_w[l]),
            "w_out": w_out[l].astype(BF16),
        }
        sconv = jnp.pad(state_conv[l].astype(F32), ((0, 0), (SUB - (CONV_WIDTH - 1), 0), (0, 0)))
        sssm = (state_ssm[l].astype(F32)
                .reshape(nb, SSM_GROUPS, SSM_GROUP_WIDTH // SSM_HEADDIM, SSM_HEADDIM, SSM_STATE)
                .transpose(0, 1, 4, 2, 3).reshape(nb, SSM_GROUPS, SSM_STATE, SSM_GROUP_WIDTH))
        shg = jnp.swapaxes(state_hgrn[l].astype(F32), -1, -2)

        h = _ffn(h, row(ln_ffa_w[l]), ffa_w_gate[l].astype(BF16), ffa_w_up[l].astype(BF16),
                 ffa_w_down[l].astype(BF16), row(ln_f_w), False)
        h, oc, os_, og = _mixer(h, params, l, n_prompt_blocks, sconv, sssm, shg)
        h = _ffn(h, row(ln_ffb_w[l]), ffb_w_gate[l].astype(BF16), ffb_w_up[l].astype(BF16),
                 ffb_w_down[l].astype(BF16), row(ln_f_w), l == depth - 1)
        conv_out.append(oc[:, :, SUB - (CONV_WIDTH - 1):, :])
        ssm_out.append(os_.reshape(2, nb, SSM_GROUPS, SSM_STATE, SSM_GROUP_WIDTH // SSM_HEADDIM, SSM_HEADDIM)
                       .transpose(0, 1, 2, 4, 5, 3)
                       .reshape(2, nb, SSM_WIDTH // SSM_HEADDIM, SSM_HEADDIM, SSM_STATE))
        hg_out.append(jnp.swapaxes(og, -1, -2))

    h = h.reshape(n_prompt_blocks + 1, nb, BLK, D_MODEL)
    y_prompt = h[1:n_prompt_blocks].transpose(1, 0, 2, 3).reshape(nb, seq, D_MODEL)
    y_sample = h[n_prompt_blocks]
    conv_all, ssm_all, hg_all = jnp.stack(conv_out), jnp.stack(ssm_out), jnp.stack(hg_out)
    dt_ = x_prompt.dtype
    return (y_prompt.astype(dt_), y_sample.astype(x_sample.dtype),
            conv_all[:, 0].astype(dt_), ssm_all[:, 0].astype(dt_), hg_all[:, 0].astype(dt_),
            conv_all[:, 1].astype(state_conv.dtype), ssm_all[:, 1].astype(state_ssm.dtype),
            hg_all[:, 1].astype(state_hgrn.dtype))
```

```python
import functools

import numpy as np
import jax
import jax.numpy as jnp
from jax import lax
from jax.experimental import pallas as pl
from jax.experimental.pallas import tpu as pltpu

F32 = jnp.float32
BF16 = jnp.bfloat16

D_MODEL = 1024
N_STREAMS = 8
BLK = 64
ROWS = N_STREAMS * BLK
N_META = 16
N_PAD = BLK - N_META
SSM_WIDTH = 512
SSM_GROUPS = 2
SSM_GROUP_WIDTH = SSM_WIDTH // SSM_GROUPS
SSM_HEADDIM = 64
SSM_STATE = 128
CONV_CH = 1024
CONV_WIDTH = 4
HG_WIDTH = 512
HG_HEADS = 4
HG_DIM = 128
D_FF = 2816
FF_CHUNK = 1408
EPS = 1e-6
LB_FLOOR = 1e-30
SUB = 8
LEVELS = (32, 16, 8)
LOG2E = 1.4426950408889634

STREAMS_PER_TRIP = 4
STAGGER = 5
LANES = 128
CONV_TILES = CONV_CH // LANES
BIG = float(2 ** 100)

C_Z, C_Q, C_F, C_I, C_G = 0, 512, 1024, 1536, 2048
IN_MAIN = 2560

VMEM_LIMIT_BYTES = 56 * 1024 * 1024


def _rms(x, w):
    return x * lax.rsqrt(jnp.mean(x * x, axis=-1, keepdims=True) + EPS) * w


def _silu(x):
    return x / (1.0 + jnp.exp(-x))


def _log1p(y):
    u = 1.0 + y
    return jnp.where(u == 1.0, y, jnp.log(u) * (y / (u - 1.0)))


def _softplus(x):
    return jnp.maximum(x, 0.0) + _log1p(jnp.exp(-jnp.abs(x)))


def _dot(a, b):
    return jnp.dot(a, b, preferred_element_type=F32)


def _dot_nt(a, b):
    return lax.dot_general(a, b, (((1,), (1,)), ((), ())), preferred_element_type=F32)


def _dot_tn(a, b):
    return lax.dot_general(a, b, (((0,), (0,)), ((), ())), preferred_element_type=F32)


def _embed_rows(piece, r0):
    s, w = piece.shape
    tile = 2 * SUB
    if s % tile:
        assert s == SUB and r0 % SUB == 0
        zeros = jnp.zeros((SUB, w), F32)
        piece = jnp.concatenate([zeros, piece] if r0 % tile else [piece, zeros], axis=0)
        r0, s = r0 - r0 % tile, tile
    assert r0 % tile == 0 and s % tile == 0
    parts = [piece.astype(BF16)]
    if r0:
        parts.insert(0, jnp.zeros((r0, w), BF16))
    if BLK - r0 - s:
        parts.append(jnp.zeros((BLK - r0 - s, w), BF16))
    return jnp.concatenate(parts, axis=0) if len(parts) > 1 else parts[0]


def _cumsum_rows(x, tri3):
    hi = x.astype(BF16)
    r1 = x - hi.astype(F32)
    mid = r1.astype(BF16)
    lo = (r1 - mid.astype(F32)).astype(BF16)
    return _dot(tri3, jnp.concatenate([hi, mid, lo], axis=0))


def _ffn_kernel(h_ref, lnw_ref, wg_ref, wu_ref, wd_ref, lnf_ref, o_ref, *, final_norm):
    h = h_ref[...]
    hn = _rms(h, lnw_ref[...]).astype(BF16)
    acc = jnp.zeros_like(h)
    for c in range(D_FF // FF_CHUNK):
        cs = slice(c * FF_CHUNK, (c + 1) * FF_CHUNK)
        g = _dot(hn, wg_ref[:, cs])
        u = _dot(hn, wu_ref[:, cs])
        a = (_silu(g) * u).astype(BF16)
        acc = acc + _dot(a, wd_ref[cs, :])
    out = h + 0.5 * acc
    if final_norm:
        out = _rms(out, lnf_ref[...])
    o_ref[...] = out


def _const_spec(shape):
    nd = len(shape)
    return pl.BlockSpec(shape, lambda i: (0,) * nd, pipeline_mode=pl.Buffered(1))


def _ffn(h, lnw, wg, wu, wd, lnf, final_norm):
    n_rows = h.shape[0]
    row_spec = pl.BlockSpec((ROWS, D_MODEL), lambda i: (i, 0))
    return pl.pallas_call(
        functools.partial(_ffn_kernel, final_norm=final_norm),
        grid=(n_rows // ROWS,),
        in_specs=[row_spec, _const_spec((1, D_MODEL)), _const_spec((D_MODEL, D_FF)),
                  _const_spec((D_MODEL, D_FF)), _const_spec((D_FF, D_MODEL)),
                  _const_spec((1, D_MODEL))],
        out_specs=row_spec,
        out_shape=jax.ShapeDtypeStruct(h.shape, F32),
        compiler_params=pltpu.CompilerParams(
            dimension_semantics=("parallel",), vmem_limit_bytes=VMEM_LIMIT_BYTES),
        name="ffn_final" if final_norm else "ffn",
    )(h, lnw, wg, wu, wd, lnf)


def _mixer_kernel(h_ref, lnw_ref, win_ref, wx_ref, wdt_ref, convw_ref, convb_ref, dtb_ref, alog_ref,
                  dskip_ref, ssmnw_ref, lbraw_ref, hgnw_ref, wout_ref, emat_ref, bmat_ref,
                  sconv_ref, sssm_ref, shg_ref,
                  hout_ref, oconv_ref, ossm_ref, ohg_ref,
                  proj_ref, xbc_ref, dtraw_ref, mix_ref, pbuf_ref, gb_ref, *, layer, n_prompt_blocks):
    step = pl.program_id(0)

    @pl.when(step == 0)
    def _():
        oconv_ref[...] = jnp.zeros(oconv_ref.shape, F32)
        ossm_ref[...] = jnp.zeros(ossm_ref.shape, F32)
        ohg_ref[...] = jnp.zeros(ohg_ref.shape, F32)

    @pl.when(step == n_prompt_blocks)
    def _():
        oconv_ref[0] = sconv_ref[...]
        ossm_ref[0] = sssm_ref[...]
        ohg_ref[0] = shg_ref[...]

    hn = _rms(h_ref[...], lnw_ref[...]).astype(BF16)
    proj_ref[...] = _dot(hn, win_ref[...])
    dtraw_ref[...] = _dot(hn, wdt_ref[...])
    x_new = _dot(hn, wx_ref[...])
    for sb in range(N_STREAMS):
        for lt in range(CONV_TILES):
            xbc_ref[sb, lt, SUB:SUB + BLK, :] = x_new[sb * BLK:(sb + 1) * BLK, lt * LANES:(lt + 1) * LANES]

    first_valid = jnp.where(step == 0, N_PAD, 0)
    t_col = lax.broadcasted_iota(jnp.int32, (BLK, 1), 0)
    valid = t_col >= first_valid

    tri3 = (lax.broadcasted_iota(jnp.int32, (BLK, 3 * BLK), 0)
            >= lax.broadcasted_iota(jnp.int32, (BLK, 3 * BLK), 1) % BLK).astype(BF16)

    conv_w = convw_ref[...]
    conv_b = convb_ref[...]
    a2 = -jnp.exp(alog_ref[...]) * LOG2E
    row_w = lax.broadcasted_iota(jnp.int32, (BLK, SSM_WIDTH), 0)
    col_w = lax.broadcasted_iota(jnp.int32, (BLK, SSM_WIDTH), 1)
    diag_pick = (col_w % BLK) == row_w
    row_g = lax.broadcasted_iota(jnp.int32, (BLK, SSM_GROUP_WIDTH), 0)
    col_g = lax.broadcasted_iota(jnp.int32, (BLK, SSM_GROUP_WIDTH), 1)
    causal_g = (col_g % BLK) <= row_g
    bd_r = lax.broadcasted_iota(jnp.int32, (SSM_GROUP_WIDTH, SSM_GROUP_WIDTH), 0)
    bd_c = lax.broadcasted_iota(jnp.int32, (SSM_GROUP_WIDTH, SSM_GROUP_WIDTH), 1)
    head_diag = ((bd_r // BLK) == (bd_c // BLK)).astype(BF16)

    raw = lbraw_ref[...]
    e_raw = jnp.exp(raw - jnp.max(raw, axis=0, keepdims=True))
    sm = e_raw / jnp.sum(e_raw, axis=0, keepdims=True)
    lb = jnp.clip(jnp.sum(sm[0:layer + 1], axis=0, keepdims=True) - sm[0:1], 0.0, 1.0 - 1e-6)
    log_lb = jnp.log(jnp.maximum(lb, LB_FLOOR))
    log_1m_lb = jnp.log1p(-lb)
    big_rows = jnp.where(lax.broadcasted_iota(jnp.int32, (BLK, HG_WIDTH), 0) == 0, BIG, 0.0).astype(BF16)
    ai =lax.broadcasted_iota(jnp.int32, (BLK, BLK), 0)
    aj = lax.broadcasted_iota(jnp.int32, (BLK, BLK), 1)
    sub_mask = (ai // SUB) == (aj // SUB)

    def per_stream(b, slot):
        r0 = b * BLK if isinstance(b, int) else pl.multiple_of(b * BLK, BLK)
        rows = pl.ds(r0, BLK)

        tail = oconv_ref[0, b]
        conv_parts = []
        for lt in range(CONV_TILES):
            ls = slice(lt * LANES, (lt + 1) * LANES)
            xbc_ref[b, lt, 0:SUB, :] = tail[:, ls]
            part = conv_b[:, ls]
            for s in range(CONV_WIDTH - 1, -1, -1):
                tap = xbc_ref[b, lt, pl.ds(SUB - s, BLK, stride=1), :]
                part = part + tap * conv_w[CONV_WIDTH - 1 - s:CONV_WIDTH - s, ls]
            conv_parts.append(part)
        oconv_ref[0, b] = jnp.concatenate(
            [xbc_ref[b, lt, BLK:BLK + SUB, :] for lt in range(CONV_TILES)], axis=1)
        xbc = _silu(jnp.concatenate(conv_parts, axis=1))
        xs = xbc[:, 0:SSM_WIDTH]
        yield

        dt = _softplus(dtraw_ref[rows, :] + dtb_ref[...])
        dt = jnp.where(valid, dt, 0.0)
        cum = _cumsum_rows(dt * a2, tri3)
        cum_row = jnp.sum(jnp.where(diag_pick, cum, 0.0), axis=0, keepdims=True)
        yield
        y_parts = []
        for g in range(SSM_GROUPS):
            cs = slice(g * SSM_GROUP_WIDTH, (g + 1) * SSM_GROUP_WIDTH)
            b_g = xbc[:, SSM_WIDTH + g * SSM_STATE:SSM_WIDTH + (g + 1) * SSM_STATE].astype(BF16)
            c_g = xbc[:, SSM_WIDTH + (SSM_GROUPS + g) * SSM_STATE:
                      SSM_WIDTH + (SSM_GROUPS + g + 1) * SSM_STATE].astype(BF16)
            cum_g = cum[:, cs]
            cbm = _dot_nt(c_g, jnp.concatenate([b_g] * 4, axis=0))
            decay = jnp.where(causal_g, jnp.exp2(cum_g - cum_row[:, cs]), 0.0)
            xs_g = xs[:, cs]
            xdt = xs_g * dt[:, cs]
            xdt_bd = jnp.concatenate([xdt.astype(BF16)] * 4, axis=0) * head_diag
            st = ossm_ref[0, b, g]
            y_g = _dot((cbm * decay).astype(BF16), xdt_bd)
            y_g = y_g + _dot(c_g, st.astype(BF16)) * jnp.exp2(cum_g)
            y_g = y_g + dskip_ref[:, cs] * xs_g
            last = cum_g[BLK - 1:BLK]
            xdt_w = (xdt * jnp.exp2(last - cum_g)).astype(BF16)
            ossm_ref[0, b, g] = jnp.exp2(last) * st + _dot_tn(b_g, xdt_w)
            y_g = y_g * _silu(proj_ref[rows, C_Z + g * SSM_GROUP_WIDTH:C_Z + (g + 1) * SSM_GROUP_WIDTH])
            y_g = y_g * lax.rsqrt(jnp.mean(y_g * y_g, axis=-1, keepdims=True) + EPS)
            y_parts.append(y_g * ssmnw_ref[:, cs])
            yield
        mix_ref[rows, 0:SSM_WIDTH] = jnp.concatenate(y_parts, axis=1).astype(BF16)

        q = _silu(proj_ref[rows, C_Q:C_Q + HG_WIDTH])
        fr = proj_ref[rows, C_F:C_F + HG_WIDTH]
        log_sig = jnp.minimum(fr, 0.0) - _log1p(jnp.exp(-jnp.abs(fr)))
        t2 = log_1m_lb + log_sig
        lf = jnp.maximum(log_lb, t2) + _log1p(jnp.exp(-jnp.abs(log_lb - t2)))
        lf = jnp.where(valid, lf, 0.0)
        log2_key = jnp.where(valid, (t2 - fr) * LOG2E, -BIG)
        v16 = proj_ref[rows, C_I:C_I + HG_WIDTH].astype(BF16)
        cumh = _cumsum_rows(lf * LOG2E, tri3)
        g = cumh - log2_key
        lasth = cumh[BLK - 1:BLK]
        q_in = (q * jnp.exp2(cumh)).astype(BF16)
        k_out = jnp.exp2(lasth - g).astype(BF16)
        yield

        q_cols, k_cols = [], []
        for s in LEVELS:
            for m in range(0, BLK, 2 * s):
                lo, up = slice(m, m + s), slice(m + s, m + 2 * s)
                mid = cumh[m + s - 1:m + s]
                k_cols.append(_embed_rows(jnp.exp2(mid - g[lo]), m))
                q_cols.append(_embed_rows(q[up] * jnp.exp2(cumh[up] - mid), m + s))
        att_off = []
        for hh in range(HG_HEADS):
            hs = slice(hh * HG_DIM, (hh + 1) * HG_DIM)
            att_off.append(_dot_nt(jnp.concatenate([c[:, hs] for c in q_cols], axis=1),
                                   jnp.concatenate([c[:, hs] for c in k_cols], axis=1)))
        yield
        g_hi = g.astype(BF16)
        g_r1 = g - g_hi.astype(F32)
        g_mid = g_r1.astype(BF16)
        g_lo = (g_r1 - g_mid.astype(F32)).astype(BF16)
        gb_ref[slot] = _dot(bmat_ref[...], jnp.concatenate([g_hi, g_mid, g_lo, big_rows], axis=0))
        yield
        for sc in range(0, BLK // SUB, 2):
            for j in range(SUB):
                slabs = []
                for u in (sc, sc + 1):
                    rs = slice(u * SUB, (u + 1) * SUB)
                    g_j = gb_ref[slot, u * BLK + j * SUB:u * BLK + (j + 1) * SUB, :]
                    slabs.append(q[rs] * jnp.exp2(cumh[rs] - g_j))
                pbuf_ref[slot, sc * SUB:(sc + 2) * SUB, j * HG_WIDTH:(j + 1) * HG_WIDTH] = (
                    jnp.concatenate(slabs, axis=0).astype(BF16))
            yield
        att_d = _dot(pbuf_ref[slot], emat_ref[...])
        yield
        o_parts = []
        for hh in range(HG_HEADS):
            hs = slice(hh * HG_DIM, (hh + 1) * HG_DIM)
            a_h = att_off[hh] + jnp.where(sub_mask, att_d[:, hh * BLK:(hh + 1) * BLK], 0.0)
            st = ohg_ref[0, b, hh]
            o_parts.append(_dot(a_h.astype(BF16), v16[:, hs]) + _dot_nt(q_in[:, hs], st.astype(BF16)))
            ohg_ref[0, b, hh] = st * jnp.exp2(lasth[:, hs]) + _dot_tn(v16[:, hs], k_out[:, hs])
            yield
        o_parts = [o_h * lax.rsqrt(jnp.mean(o_h * o_h, axis=-1, keepdims=True) + EPS) for o_h in o_parts]
        o = jnp.concatenate(o_parts, axis=1) * hgnw_ref[...]
        o = o * _silu(proj_ref[rows, C_G:C_G + HG_WIDTH])
        mix_ref[rows, SSM_WIDTH:SSM_WIDTH + HG_WIDTH] = o.astype(BF16)

    def per_group(p, carry):
        gens = [per_stream(STREAMS_PER_TRIP * p + k, k) for k in range(STREAMS_PER_TRIP)]
        started, rnd = 0, 0
        active = []
        while active or started < STREAMS_PER_TRIP:
            if started < STREAMS_PER_TRIP and rnd >= started * STAGGER:
                active.append(gens[started])
                started += 1
            active = [gen for gen in active if next(gen, True) is None]
            rnd += 1
        return carry

    lax.fori_loop(0, N_STREAMS // STREAMS_PER_TRIP, per_group, 0)
    hout_ref[...] = h_ref[...] + _dot(mix_ref[...], wout_ref[...])


def _diag_reduce_matrix():
    r = np.arange(SUB * HG_WIDTH)
    c = np.arange(HG_HEADS * BLK)
    j, head_r = r // HG_WIDTH, (r % HG_WIDTH) // HG_DIM
    head_c, cc = c // BLK, c % BLK
    m = (head_r[:, None] == head_c[None, :]) & (j[:, None] == (cc % SUB)[None, :])
    return jnp.asarray(m, dtype=BF16)


def _row_broadcast_matrix():
    r = np.arange(SUB * BLK)
    u, j, i = r // BLK, (r % BLK) // SUB, r % SUB
    c = np.arange(4 * BLK)
    part, t = c // BLK, c % BLK
    pick = (part[None, :] < 3) & (t[None, :] == (u * SUB + j)[:, None])
    mask = (part[None, :] == 3) & (t[None, :] == 0) & (i < j)[:, None]
    return jnp.asarray(pick | mask, dtype=BF16)


def _mixer(h, p, layer, n_prompt_blocks, sconv, sssm, shg):
    n_steps = h.shape[0] // ROWS
    row_spec = pl.BlockSpec((ROWS, D_MODEL), lambda i: (i, 0))
    group = lambda i: i // n_prompt_blocks
    conv_shape = (2, N_STREAMS, SUB, CONV_CH)
    ssm_shape = (2, N_STREAMS, SSM_GROUPS, SSM_STATE, SSM_GROUP_WIDTH)
    hg_shape = (2, N_STREAMS, HG_HEADS, HG_DIM, HG_DIM)
    in_specs = [
        row_spec,
        _const_spec((1, D_MODEL)),
        _const_spec((D_MODEL, IN_MAIN)),
        _const_spec((D_MODEL, CONV_CH)),
        _const_spec((D_MODEL, SSM_WIDTH)),
        _const_spec((CONV_WIDTH, CONV_CH)),
        _const_spec((1, CONV_CH)),
        _const_spec((1, SSM_WIDTH)),
        _const_spec((1, SSM_WIDTH)),
        _const_spec((1, SSM_WIDTH)),
        _const_spec((1, SSM_WIDTH)),
        _const_spec((p["lb_raw"].shape[0], HG_WIDTH)),
        _const_spec((1, HG_WIDTH)),
        _const_spec((D_MODEL, D_MODEL)),
        _const_spec((SUB * HG_WIDTH, HG_HEADS * BLK)),
        _const_spec((SUB * BLK, 4 * BLK)),
        _const_spec(conv_shape[1:]),
        _const_spec(ssm_shape[1:]),
        _const_spec(hg_shape[1:]),
    ]
    out_specs = [
        row_spec,
        pl.BlockSpec((1,) + conv_shape[1:], lambda i: (group(i), 0, 0, 0)),
        pl.BlockSpec((1,) + ssm_shape[1:], lambda i: (group(i), 0, 0, 0, 0)),
        pl.BlockSpec((1,) + hg_shape[1:], lambda i: (group(i), 0, 0, 0, 0)),
    ]
    out_shape = [
        jax.ShapeDtypeStruct(h.shape, F32),
        jax.ShapeDtypeStruct(conv_shape, F32),
        jax.ShapeDtypeStruct(ssm_shape, F32),
        jax.ShapeDtypeStruct(hg_shape, F32),
    ]
    return pl.pallas_call(
        functools.partial(_mixer_kernel, layer=layer, n_prompt_blocks=n_prompt_blocks),
        grid=(n_steps,),
        in_specs=in_specs,
        out_specs=out_specs,
        out_shape=out_shape,
        scratch_shapes=[
            pltpu.VMEM((ROWS, IN_MAIN), F32),
            pltpu.VMEM((N_STREAMS, CONV_TILES, SUB + BLK, LANES), F32),
            pltpu.VMEM((ROWS, SSM_WIDTH), F32),
            pltpu.VMEM((ROWS, D_MODEL), BF16),
            pltpu.VMEM((STREAMS_PER_TRIP, BLK, SUB * HG_WIDTH), BF16),
            pltpu.VMEM((STREAMS_PER_TRIP, SUB * BLK, HG_WIDTH), F32),
        ],
        compiler_params=pltpu.CompilerParams(
            dimension_semantics=("arbitrary",), vmem_limit_bytes=VMEM_LIMIT_BYTES),
        name="mixer",
    )(h, p["ln_mix"], p["w_in"], p["w_x"], p["w_dt"], p["conv_w"], p["conv_b"], p["dt_bias"], p["a_log"],
      p["d_skip"], p["ssm_norm"], p["lb_raw"], p["hg_norm"], p["w_out"], _diag_reduce_matrix(),
      _row_broadcast_matrix(), sconv, sssm, shg)


def _expand_heads(x):
    return jnp.repeat(x.astype(F32), SSM_HEADDIM)[None, :]


def kernel(x_prompt, x_sample, state_conv, state_ssm, state_hgrn, meta_tokens, ln_ffa_w, ffa_w_gate, ffa_w_up, ffa_w_down, ln_mix_w, w_in, conv_w, conv_b, dt_bias, a_log, d_skip, ssm_norm_w, hg_lb_raw, hg_norm_w, w_out, ln_ffb_w, ffb_w_gate, ffb_w_up, ffb_w_down, ln_f_w):
    depth = w_in.shape[0]
    nb, seq, _ = x_prompt.shape
    assert nb == N_STREAMS and x_sample.shape[:2] == (N_STREAMS, BLK) and seq % BLK == 0
    n_seq_blocks = seq // BLK
    n_prompt_blocks = n_seq_blocks + 1

    first = jnp.concatenate(
        [jnp.zeros((nb, N_PAD, D_MODEL), F32),
         jnp.broadcast_to(meta_tokens.astype(F32)[None], (nb, N_META, D_MODEL))], axis=1)
    xp = x_prompt.astype(F32).reshape(nb, n_seq_blocks, BLK, D_MODEL).transpose(1, 0, 2, 3)
    h = jnp.concatenate([first[None], xp, x_sample.astype(F32)[None]], axis=0)
    h = h.reshape((n_prompt_blocks + 1) * ROWS, D_MODEL)

    row = lambda a: a.astype(F32)[None, :]
    dt_cols = slice(SSM_WIDTH + CONV_CH, SSM_WIDTH + CONV_CH + SSM_WIDTH // SSM_HEADDIM)
    conv_out, ssm_out, hg_out = [], [], []
    for l in range(depth):
        w = w_in[l]
        params = {
            "ln_mix": row(ln_mix_w[l]),
            "w_in": jnp.concatenate([w[:, :SSM_WIDTH], w[:, dt_cols.stop:]], axis=1).astype(BF16),
            "w_x": w[:, SSM_WIDTH:dt_cols.start].astype(BF16),
            "w_dt": jnp.repeat(w[:, dt_cols], SSM_HEADDIM, axis=1).astype(BF16),
            "conv_w": conv_w[l].astype(F32),
            "conv_b": row(conv_b[l]),
            "dt_bias": _expand_heads(dt_bias[l]),
            "a_log": _expand_heads(a_log[l]),
            "d_skip": _expand_heads(d_skip[l]),
            "ssm_norm": row(ssm_norm_w[l]),
            "lb_raw": hg_lb_raw.astype(F32),
            "hg_norm": row(hg_norm_w[l]),
            "w_out": w_out[l].astype(BF16),
        }
        sconv = jnp.pad(state_conv[l].astype(F32), ((0, 0), (SUB - (CONV_WIDTH - 1), 0), (0, 0)))
        sssm = (state_ssm[l].astype(F32)
                .reshape(nb, SSM_GROUPS, SSM_GROUP_WIDTH // SSM_HEADDIM, SSM_HEADDIM, SSM_STATE)
                .transpose(0, 1, 4, 2, 3).reshape(nb, SSM_GROUPS, SSM_STATE, SSM_GROUP_WIDTH))
        shg = jnp.swapaxes(state_hgrn[l].astype(F32), -1, -2)

        h = _ffn(h, row(ln_ffa_w[l]), ffa_w_gate[l].astype(BF16), ffa_w_up[l].astype(BF16),
                 ffa_w_down[l].astype(BF16), row(ln_f_w), False)
        h, oc, os_, og = _mixer(h, params, l, n_prompt_blocks, sconv, sssm, shg)
        h = _ffn(h, row(ln_ffb_w[l]), ffb_w_gate[l].astype(BF16), ffb_w_up[l].astype(BF16),
                 ffb_w_down[l].astype(BF16), row(ln_f_w), l == depth - 1)
        conv_out.append(oc[:, :, SUB - (CONV_WIDTH - 1):, :])
        ssm_out.append(os_.reshape(2, nb, SSM_GROUPS, SSM_STATE, SSM_GROUP_WIDTH // SSM_HEADDIM, SSM_HEADDIM)
                       .transpose(0, 1, 2, 4, 5, 3)
                       .reshape(2, nb, SSM_WIDTH // SSM_HEADDIM, SSM_HEADDIM, SSM_STATE))
        hg_out.append(jnp.swapaxes(og, -1, -2))

    h = h.reshape(n_prompt_blocks + 1, nb, BLK, D_MODEL)
    y_prompt = h[1:n_prompt_blocks].transpose(1, 0, 2, 3).reshape(nb, seq, D_MODEL)
    y_sample = h[n_prompt_blocks]
    conv_all, ssm_all, hg_all = jnp.stack(conv_out), jnp.stack(ssm_out), jnp.stack(hg_out)
    dt_ = x_prompt.dtype
    return (y_prompt.astype(dt_), y_sample.astype(x_sample.dtype),
            conv_all[:, 0].astype(dt_), ssm_all[:, 0].astype(dt_), hg_all[:, 0].astype(dt_),
            conv_all[:, 1].astype(state_conv.dtype), ssm_all[:, 1].astype(state_ssm.dtype),
            hg_all[:, 1].astype(state_hgrn.dtype))
```

```python
import functools

import numpy as np
import jax
import jax.numpy as jnp
from jax import lax
from jax.experimental import pallas as pl
from jax.experimental.pallas import tpu as pltpu

F32 = jnp.float32
BF16 = jnp.bfloat16

D_MODEL = 1024
N_STREAMS = 8
BLK = 64
ROWS = N_STREAMS * BLK
N_META = 16
N_PAD = BLK - N_META
SSM_WIDTH = 512
SSM_GROUPS = 2
SSM_GROUP_WIDTH = SSM_WIDTH // SSM_GROUPS
SSM_HEADDIM = 64
SSM_STATE = 128
CONV_CH = 1024
CONV_WIDTH = 4
HG_WIDTH = 512
HG_HEADS = 4
HG_DIM = 128
D_FF = 2816
FF_CHUNK = 1408
EPS = 1e-6
LB_FLOOR = 1e-30
SUB = 8
LEVELS = (32, 16, 8)
LOG2E = 1.4426950408889634

N_PARTS = 2
STREAMS_PER_PART = N_STREAMS // N_PARTS
PART_ROWS = STREAMS_PER_PART * BLK
PROJ_CHUNK = 512
SLAB_SLOTS = 4
STAGGER = 5
LANES = 128
CONV_TILES = CONV_CH // LANES
BIG = float(2 ** 100)

C_Z, C_Q, C_F, C_I, C_G = 0, 512, 1024, 1536, 2048
IN_MAIN = 2560

VMEM_LIMIT_BYTES = 56 * 1024 * 1024


def _rms(x, w):
    return x * lax.rsqrt(jnp.mean(x * x, axis=-1, keepdims=True) + EPS) * w


def _silu(x):
    return x / (1.0 + jnp.exp(-x))


def _log1p(y):
    u = 1.0 + y
    return jnp.where(u == 1.0, y, jnp.log(u) * (y / (u - 1.0)))


def _softplus(x):
    return jnp.maximum(x, 0.0) + _log1p(jnp.exp(-jnp.abs(x)))


def _dot(a, b):
    return jnp.dot(a, b, preferred_element_type=F32)


def _dot_nt(a, b):
    return lax.dot_general(a, b, (((1,), (1,)), ((), ())), preferred_element_type=F32)


def _dot_tn(a, b):
    return lax.dot_general(a, b, (((0,), (0,)), ((), ())), preferred_element_type=F32)


def _embed_rows(piece, r0):
    s, w = piece.shape
    tile = 2 * SUB
    if s % tile:
        assert s == SUB and r0 % SUB == 0
        zeros = jnp.zeros((SUB, w), F32)
        piece = jnp.concatenate([zeros, piece] if r0 % tile else [piece, zeros], axis=0)
        r0, s = r0 - r0 % tile, tile
    assert r0 % tile == 0 and s % tile == 0
    parts = [piece.astype(BF16)]
    if r0:
        parts.insert(0, jnp.zeros((r0, w), BF16))
    if BLK - r0 - s:
        parts.append(jnp.zeros((BLK - r0 - s, w), BF16))
    return jnp.concatenate(parts, axis=0) if len(parts) > 1 else parts[0]


def _cumsum_rows(x, tri3):
    hi = x.astype(BF16)
    r1 = x - hi.astype(F32)
    mid = r1.astype(BF16)
    lo = (r1 - mid.astype(F32)).astype(BF16)
    return _dot(tri3, jnp.concatenate([hi, mid, lo], axis=0))


def _swiglu_step(h, lnw_ref, wg_ref, wu_ref, wd_ref):
    hn = _rms(h, lnw_ref[...]).astype(BF16)
    acc = jnp.zeros_like(h)
    for c in range(D_FF // FF_CHUNK):
        cs = slice(c * FF_CHUNK, (c + 1) * FF_CHUNK)
        g = _dot(hn, wg_ref[:, cs])
        u = _dot(hn, wu_ref[:, cs])
        a = (_silu(g) * u).astype(BF16)
        acc = acc + _dot(a, wd_ref[cs, :])
    return h + 0.5 * acc


def _ffn_kernel(h_ref, lnw_ref, wg_ref, wu_ref, wd_ref, o_ref):
    o_ref[...] = _swiglu_step(h_ref[...], lnw_ref, wg_ref, wu_ref, wd_ref)


def _ffn_first_kernel(xp_ref, xs_ref, meta_ref, lnw_ref, wg_ref, wu_ref, wd_ref, o_ref):
    step = pl.program_id(0)
    first = jnp.concatenate([jnp.zeros((N_PAD, D_MODEL), F32), meta_ref[...]], axis=0)
    h = jnp.where(step == 0, first[None], xp_ref[...])
    h = jnp.where(step == pl.num_programs(0) - 1, xs_ref[...], h)
    o_ref[...] = _swiglu_step(h.reshape(ROWS, D_MODEL), lnw_ref, wg_ref, wu_ref, wd_ref)


def _ffn_last_kernel(h_ref, lnw_ref, wg_ref, wu_ref, wd_ref, lnf_ref, yp_ref, ys_ref):
    step = pl.program_id(0)
    last = pl.num_programs(0) - 1
    y = _rms(_swiglu_step(h_ref[...], lnw_ref, wg_ref, wu_ref, wd_ref), lnf_ref[...])
    y = y.reshape(N_STREAMS, BLK, D_MODEL)

    @pl.when(step < last)
    def _():
        yp_ref[...] = y

    @pl.when(step == last)
    def _():
        ys_ref[...] = y


def _const_spec(shape):
    nd = len(shape)
    return pl.BlockSpec(shape, lambda i: (0,) * nd, pipeline_mode=pl.Buffered(1))


_ROW_SPEC = pl.BlockSpec((ROWS, D_MODEL), lambda i: (i, 0))
_FFN_WEIGHT_SPECS = [_const_spec((1, D_MODEL)), _const_spec((D_MODEL, D_FF)),
                     _const_spec((D_MODEL, D_FF)), _const_spec((D_FF, D_MODEL))]
_FFN_PARAMS = pltpu.CompilerParams(dimension_semantics=("arbitrary",), vmem_limit_bytes=VMEM_LIMIT_BYTES)


def _prompt_block_spec(n_seq_blocks):
    return pl.BlockSpec((N_STREAMS, BLK, D_MODEL),
                        lambda i: (0, jnp.clip(i - 1, 0, n_seq_blocks - 1), 0))


def _ffn(h, lnw, wg, wu, wd):
    return pl.pallas_call(
        _ffn_kernel,
        grid=(h.shape[0] // ROWS,),
        in_specs=[_ROW_SPEC] + _FFN_WEIGHT_SPECS,
        out_specs=_ROW_SPEC,
        out_shape=jax.ShapeDtypeStruct(h.shape, F32),
        compiler_params=_FFN_PARAMS,
        name="ffn",
    )(h, lnw, wg, wu, wd)


def _ffn_first(x_prompt, x_sample, meta, lnw, wg, wu, wd):
    n_seq_blocks = x_prompt.shape[1] // BLK
    n_steps = n_seq_blocks + 2
    return pl.pallas_call(
        _ffn_first_kernel,
        grid=(n_steps,),
        in_specs=[_prompt_block_spec(n_seq_blocks), _const_spec((N_STREAMS, BLK, D_MODEL)),
                  _const_spec((N_META, D_MODEL))] + _FFN_WEIGHT_SPECS,
        out_specs=_ROW_SPEC,
        out_shape=jax.ShapeDtypeStruct((n_steps * ROWS, D_MODEL), F32),
        compiler_params=_FFN_PARAMS,
        name="ffn_first",
    )(x_prompt, x_sample, meta, lnw, wg, wu, wd)


def _ffn_last(h, lnw, wg, wu, wd, lnf):
    n_steps = h.shape[0] // ROWS
    n_seq_blocks = n_steps - 2
    return pl.pallas_call(
        _ffn_last_kernel,
        grid=(n_steps,),
        in_specs=[_ROW_SPEC] + _FFN_WEIGHT_SPECS + [_const_spec((1, D_MODEL))],
        out_specs=[_prompt_block_spec(n_seq_blocks),
                   pl.BlockSpec((N_STREAMS, BLK, D_MODEL), lambda i: (0, 0, 0))],
        out_shape=[jax.ShapeDtypeStruct((N_STREAMS, n_seq_blocks * BLK, D_MODEL), F32),
                   jax.ShapeDtypeStruct((N_STREAMS, BLK, D_MODEL), F32)],
        compiler_params=_FFN_PARAMS,
        name="ffn_last",
    )(h, lnw, wg, wu, wd, lnf)


def _mixer_kernel(h_ref, lnw_ref, win_ref, wx_ref, wdt_ref, convw_ref, convb_ref, dtb_ref, alog_ref,
                  dskip_ref, ssmnw_ref, lbraw_ref, hgnw_ref, wout_ref, hsum_ref, bmat_ref,
                  sconv_ref, sssm_ref, shg_ref,
                  hout_ref, oconv_ref, ossm_ref, ohg_ref,
                  proj_ref, xbc_ref, dtraw_ref, mix_ref, pbuf_ref, gb_ref, *, layer, n_prompt_blocks):
    step = pl.program_id(0)

    @pl.when(step == 0)
    def _():
        oconv_ref[...] = jnp.zeros(oconv_ref.shape, F32)
        ossm_ref[...] = jnp.zeros(ossm_ref.shape, F32)
        ohg_ref[...] = jnp.zeros(ohg_ref.shape, F32)

    @pl.when(step == n_prompt_blocks)
    def _():
        oconv_ref[0] = sconv_ref[...]
        ossm_ref[0] = sssm_ref[...]
        ohg_ref[0] = shg_ref[...]

    def in_projection(part):
        r0 = part * PART_ROWS
        rs = slice(r0, r0 + PART_ROWS)
        hn = _rms(h_ref[rs, :], lnw_ref[...]).astype(BF16)
        for c0 in range(0, IN_MAIN, PROJ_CHUNK):
            proj_ref[rs, c0:c0 + PROJ_CHUNK] = _dot(hn, win_ref[:, c0:c0 + PROJ_CHUNK])
            yield
        dtraw_ref[rs, :] = _dot(hn, wdt_ref[...])
        yield
        for c0 in range(0, CONV_CH, PROJ_CHUNK):
            x_new = _dot(hn, wx_ref[:, c0:c0 + PROJ_CHUNK])
            for k in range(STREAMS_PER_PART):
                for lt in range(PROJ_CHUNK // LANES):
                    xbc_ref[part * STREAMS_PER_PART + k, c0 // LANES + lt, SUB:SUB + BLK, :] = (
                        x_new[k * BLK:(k + 1) * BLK, lt * LANES:(lt + 1) * LANES])
            yield

    def out_projection(part):
        r0 = part * PART_ROWS
        rs = slice(r0, r0 + PART_ROWS)
        for c0 in range(0, D_MODEL, PROJ_CHUNK):
            cs = slice(c0, c0 + PROJ_CHUNK)
            hout_ref[rs, cs] = h_ref[rs, cs] + _dot(mix_ref[rs, :], wout_ref[:, cs])
            yield

    first_valid = jnp.where(step == 0, N_PAD, 0)
    t_col = lax.broadcasted_iota(jnp.int32, (BLK, 1), 0)
    valid = t_col >= first_valid

    tri3 = (lax.broadcasted_iota(jnp.int32, (BLK, 3 * BLK), 0)
            >= lax.broadcasted_iota(jnp.int32, (BLK, 3 * BLK), 1) % BLK).astype(BF16)

    conv_w = convw_ref[...]
    conv_b = convb_ref[...]
    a2 = -jnp.exp(alog_ref[...]) * LOG2E
    row_w = lax.broadcasted_iota(jnp.int32, (BLK, SSM_WIDTH), 0)
    col_w = lax.broadcasted_iota(jnp.int32, (BLK, SSM_WIDTH), 1)
    diag_pick = (col_w % BLK) == row_w
    row_g = lax.broadcasted_iota(jnp.int32, (BLK, SSM_GROUP_WIDTH), 0)
    col_g = lax.broadcasted_iota(jnp.int32, (BLK, SSM_GROUP_WIDTH), 1)
    causal_g = (col_g % BLK) <= row_g
    bd_r = lax.broadcasted_iota(jnp.int32, (SSM_GROUP_WIDTH, SSM_GROUP_WIDTH), 0)
    bd_c = lax.broadcasted_iota(jnp.int32, (SSM_GROUP_WIDTH, SSM_GROUP_WIDTH), 1)
    head_diag = ((bd_r // BLK) == (bd_c // BLK)).astype(BF16)

    raw = lbraw_ref[...]
    e_raw = jnp.exp(raw - jnp.max(raw, axis=0, keepdims=True))
    sm = e_raw / jnp.sum(e_raw, axis=0, keepdims=True)
    lb = jnp.clip(jnp.sum(sm[0:layer + 1], axis=0, keepdims=True) - sm[0:1], 0.0, 1.0 - 1e-6)
    log_lb = jnp.log(jnp.maximum(lb, LB_FLOOR))
    log_1m_lb = jnp.log1p(-lb)
    big_rows = jnp.where(lax.broadcasted_iota(jnp.int32, (BLK, HG_WIDTH), 0) == 0, BIG, 0.0).astype(BF16)
    di = lax.broadcasted_iota(jnp.int32, (BLK, HG_HEADS * BLK), 0)
    dc = lax.broadcasted_iota(jnp.int32, (BLK, HG_HEADS * BLK), 1) % BLK
    slab_masks = [((dc % SUB) == j) & ((dc // SUB) == (di // SUB)) for j in range(SUB)]

    def per_stream(b, slot):
        r0 = b * BLK if isinstance(b, int) else pl.multiple_of(b * BLK, BLK)
        rows = pl.ds(r0, BLK)

        tail = oconv_ref[0, b]
        conv_parts = []
        for lt in range(CONV_TILES):
            ls = slice(lt * LANES, (lt + 1) * LANES)
            xbc_ref[b, lt, 0:SUB, :] = tail[:, ls]
            part = conv_b[:, ls]
            for s in range(CONV_WIDTH - 1, -1, -1):
                tap = xbc_ref[b, lt, pl.ds(SUB - s, BLK, stride=1), :]
                part = part + tap * conv_w[CONV_WIDTH - 1 - s:CONV_WIDTH - s, ls]
            conv_parts.append(part)
        oconv_ref[0, b] = jnp.concatenate(
            [xbc_ref[b, lt, BLK:BLK + SUB, :] for lt in range(CONV_TILES)], axis=1)
        xbc = _silu(jnp.concatenate(conv_parts, axis=1))
        xs = xbc[:, 0:SSM_WIDTH]
        yield

        dt = _softplus(dtraw_ref[rows, :] + dtb_ref[...])
        dt = jnp.where(valid, dt, 0.0)
        cum = _cumsum_rows(dt * a2, tri3)
        cum_row = jnp.sum(jnp.where(diag_pick, cum, 0.0), axis=0, keepdims=True)
        yield
        y_parts = []
        for g in range(SSM_GROUPS):
            cs = slice(g * SSM_GROUP_WIDTH, (g + 1) * SSM_GROUP_WIDTH)
            b_g = xbc[:, SSM_WIDTH + g * SSM_STATE:SSM_WIDTH + (g + 1) * SSM_STATE].astype(BF16)
            c_g = xbc[:, SSM_WIDTH + (SSM_GROUPS + g) * SSM_STATE:
                      SSM_WIDTH + (SSM_GROUPS + g + 1) * SSM_STATE].astype(BF16)
            cum_g = cum[:, cs]
            cbm = _dot_nt(c_g, jnp.concatenate([b_g] * 4, axis=0))
            decay = jnp.where(causal_g, jnp.exp2(cum_g - cum_row[:, cs]), 0.0)
            xs_g = xs[:, cs]
            xdt = xs_g * dt[:, cs]
            xdt_bd = jnp.concatenate([xdt.astype(BF16)] * 4, axis=0) * head_diag
            st = ossm_ref[0, b, g]
            y_g = _dot((cbm * decay).astype(BF16), xdt_bd)
            y_g = y_g + _dot(c_g, st.astype(BF16)) * jnp.exp2(cum_g)
            y_g = y_g + dskip_ref[:, cs] * xs_g
            last = cum_g[BLK - 1:BLK]
            xdt_w = (xdt * jnp.exp2(last - cum_g)).astype(BF16)
            ossm_ref[0, b, g] = jnp.exp2(last) * st + _dot_tn(b_g, xdt_w)
            y_g = y_g * _silu(proj_ref[rows, C_Z + g * SSM_GROUP_WIDTH:C_Z + (g + 1) * SSM_GROUP_WIDTH])
            y_g = y_g * lax.rsqrt(jnp.mean(y_g * y_g, axis=-1, keepdims=True) + EPS)
            y_parts.append(y_g * ssmnw_ref[:, cs])
            yield
        mix_ref[rows, 0:SSM_WIDTH] = jnp.concatenate(y_parts, axis=1).astype(BF16)

        q = _silu(proj_ref[rows, C_Q:C_Q + HG_WIDTH])
        fr = proj_ref[rows, C_F:C_F + HG_WIDTH]
        log_sig = jnp.minimum(fr, 0.0) - jnp.log(1.0 + jnp.exp(-jnp.abs(fr)))
        t2 = log_1m_lb + log_sig
        lf = jnp.maximum(log_lb, t2) + jnp.log(1.0 + jnp.exp(-jnp.abs(log_lb - t2)))
        lf = jnp.where(valid, lf, 0.0)
        log2_key = jnp.where(valid, (t2 - fr) * LOG2E, -BIG)
        v16 = proj_ref[rows, C_I:C_I + HG_WIDTH].astype(BF16)
        cumh = _cumsum_rows(lf * LOG2E, tri3)
        g = cumh - log2_key
        lasth = cumh[BLK - 1:BLK]
        q_in = (q * jnp.exp2(cumh)).astype(BF16)
        k_out = jnp.exp2(lasth - g).astype(BF16)
        yield

        q_cols, k_cols = [], []
        for s in LEVELS:
            for m in range(0, BLK, 2 * s):
                lo, up = slice(m, m + s), slice(m + s, m + 2 * s)
                mid = cumh[m + s - 1:m + s]
                k_cols.append(_embed_rows(jnp.exp2(mid - g[lo]), m))
                q_cols.append(_embed_rows(q[up] * jnp.exp2(cumh[up] - mid), m + s))
        att_off = []
        for hh in range(HG_HEADS):
            hs = slice(hh * HG_DIM, (hh + 1) * HG_DIM)
            att_off.append(_dot_nt(jnp.concatenate([c[:, hs] for c in q_cols], axis=1),
                                   jnp.concatenate([c[:, hs] for c in k_cols], axis=1)))
        yield
        g_hi = g.astype(BF16)
        g_r1 = g - g_hi.astype(F32)
        g_mid = g_r1.astype(BF16)
        g_lo = (g_r1 - g_mid.astype(F32)).astype(BF16)
        gb_ref[slot] = _dot(bmat_ref[...], jnp.concatenate([g_hi, g_mid, g_lo, big_rows], axis=0))
        yield
        for sc in range(0, BLK // SUB, 2):
            for j in range(SUB):
                slabs = []
                for u in (sc, sc + 1):
                    rs = slice(u * SUB, (u + 1) * SUB)
                    g_j = gb_ref[slot, u * BLK + j * SUB:u * BLK + (j + 1) * SUB, :]
                    slabs.append(q[rs] * jnp.exp2(cumh[rs] - g_j))
                pbuf_ref[slot, j * BLK + sc * SUB:j * BLK + (sc + 2) * SUB, :] = (
                    jnp.concatenate(slabs, axis=0).astype(BF16))
            yield
        slab_sums = _dot(pbuf_ref[slot], hsum_ref[...])
        yield
        att_d = jnp.zeros((BLK, HG_HEADS * BLK), F32)
        for j in range(SUB):
            att_d = jnp.where(slab_masks[j], slab_sums[j * BLK:(j + 1) * BLK], att_d)
        decay_rows = jnp.exp2(lasth)
        o_parts = []
        for hh in range(HG_HEADS):
            hs = slice(hh * HG_DIM, (hh + 1) * HG_DIM)
            a_h = (att_off[hh] + att_d[:, hh * BLK:(hh + 1) * BLK]).astype(BF16)
            st = ohg_ref[0, b, hh]
            o_parts.append(_dot(jnp.concatenate([q_in[:, hs], a_h], axis=1),
                                jnp.concatenate([st.astype(BF16), v16[:, hs]], axis=0)))
            decay_cols = jnp.broadcast_to(decay_rows[:, hs], (HG_DIM, HG_DIM)).T
            ohg_ref[0, b, hh] = st * decay_cols + _dot_tn(k_out[:, hs], v16[:, hs])
            yield
        o_parts = [o_h * lax.rsqrt(jnp.mean(o_h * o_h, axis=-1, keepdims=True) + EPS) for o_h in o_parts]
        o = jnp.concatenate(o_parts, axis=1) * hgnw_ref[...]
        o = o * _silu(proj_ref[rows, C_G:C_G + HG_WIDTH])
        mix_ref[rows, SSM_WIDTH:SSM_WIDTH + HG_WIDTH] = o.astype(BF16)

    tasks = {}
    for part in range(N_PARTS):
        tasks["in%d" % part] = (in_projection(part), [] if part == 0 else ["in%d" % (part - 1)])
        first = part * STREAMS_PER_PART
        for b in range(first, first + STREAMS_PER_PART):
            tasks["s%d" % b] = (per_stream(b, b % SLAB_SLOTS), ["in%d" % part])
        tasks["out%d" % part] = (out_projection(part), ["s%d" % b for b in range(first, first + STREAMS_PER_PART)])
    done, active, rnd, last_stream_start = set(), [], 0, -STAGGER
    pending = list(tasks)
    while pending or active:
        for name in list(pending):
            is_stream = name.startswith("s")
            if all(dep in done for dep in tasks[name][1]) and not (is_stream and rnd < last_stream_start + STAGGER):
                pending.remove(name)
                active.append(name)
                if is_stream:
                    last_stream_start = rnd
        for name in list(active):
            if next(tasks[name][0], True) is not None:
                active.remove(name)
                done.add(name)
        rnd += 1


def _head_sum_matrix():
    head_r = np.arange(HG_WIDTH) // HG_DIM
    head_c = np.arange(HG_HEADS * BLK) // BLK
    return jnp.asarray(head_r[:, None] == head_c[None, :], dtype=BF16)


def _row_broadcast_matrix():
    r = np.arange(SUB * BLK)
    u, j, i = r // BLK, (r % BLK) // SUB, r % SUB
    c = np.arange(4 * BLK)
    part, t = c // BLK, c % BLK
    pick = (part[None, :] < 3) & (t[None, :] == (u * SUB + j)[:, None])
    mask = (part[None, :] == 3) & (t[None, :] == 0) & (i < j)[:, None]
    return jnp.asarray(pick | mask, dtype=BF16)


def _mixer(h, p, layer, n_prompt_blocks, sconv, sssm, shg):
    n_steps = h.shape[0] // ROWS
    row_spec = pl.BlockSpec((ROWS, D_MODEL), lambda i: (i, 0))
    group = lambda i: i // n_prompt_blocks
    conv_shape = (2, N_STREAMS, SUB, CONV_CH)
    ssm_shape = (2, N_STREAMS, SSM_GROUPS, SSM_STATE, SSM_GROUP_WIDTH)
    hg_shape = (2, N_STREAMS, HG_HEADS, HG_DIM, HG_DIM)
    in_specs = [
        row_spec,
        _const_spec((1, D_MODEL)),
        _const_spec((D_MODEL, IN_MAIN)),
        _const_spec((D_MODEL, CONV_CH)),
        _const_spec((D_MODEL, SSM_WIDTH)),
        _const_spec((CONV_WIDTH, CONV_CH)),
        _const_spec((1, CONV_CH)),
        _const_spec((1, SSM_WIDTH)),
        _const_spec((1, SSM_WIDTH)),
        _const_spec((1, SSM_WIDTH)),
        _const_spec((1, SSM_WIDTH)),
        _const_spec((p["lb_raw"].shape[0], HG_WIDTH)),
        _const_spec((1, HG_WIDTH)),
        _const_spec((D_MODEL, D_MODEL)),
        _const_spec((HG_WIDTH, HG_HEADS * BLK)),
        _const_spec((SUB * BLK, 4 * BLK)),
        _const_spec(conv_shape[1:]),
        _const_spec(ssm_shape[1:]),
        _const_spec(hg_shape[1:]),
    ]
    out_specs = [
        row_spec,
        pl.BlockSpec((1,) + conv_shape[1:], lambda i: (group(i), 0, 0, 0)),
        pl.BlockSpec((1,) + ssm_shape[1:], lambda i: (group(i), 0, 0, 0, 0)),
        pl.BlockSpec((1,) + hg_shape[1:], lambda i: (group(i), 0, 0, 0, 0)),
    ]
    out_shape = [
        jax.ShapeDtypeStruct(h.shape, F32),
        jax.ShapeDtypeStruct(conv_shape, F32),
        jax.ShapeDtypeStruct(ssm_shape, F32),
        jax.ShapeDtypeStruct(hg_shape, F32),
    ]
    return pl.pallas_call(
        functools.partial(_mixer_kernel, layer=layer, n_prompt_blocks=n_prompt_blocks),
        grid=(n_steps,),
        in_specs=in_specs,
        out_specs=out_specs,
        out_shape=out_shape,
        scratch_shapes=[
            pltpu.VMEM((ROWS, IN_MAIN), F32),
            pltpu.VMEM((N_STREAMS, CONV_TILES, SUB + BLK, LANES), F32),
            pltpu.VMEM((ROWS, SSM_WIDTH), F32),
            pltpu.VMEM((ROWS, D_MODEL), BF16),
            pltpu.VMEM((SLAB_SLOTS, SUB * BLK, HG_WIDTH), BF16),
            pltpu.VMEM((SLAB_SLOTS, SUB * BLK, HG_WIDTH), F32),
        ],
        compiler_params=pltpu.CompilerParams(
            dimension_semantics=("arbitrary",), vmem_limit_bytes=VMEM_LIMIT_BYTES),
        name="mixer",
    )(h, p["ln_mix"], p["w_in"], p["w_x"], p["w_dt"], p["conv_w"], p["conv_b"], p["dt_bias"], p["a_log"],
      p["d_skip"], p["ssm_norm"], p["lb_raw"], p["hg_norm"], p["w_out"], _head_sum_matrix(),
      _row_broadcast_matrix(), sconv, sssm, shg)


def _expand_heads(x):
    return jnp.repeat(x.astype(F32), SSM_HEADDIM)[None, :]


def kernel(x_prompt, x_sample, state_conv, state_ssm, state_hgrn, meta_tokens, ln_ffa_w, ffa_w_gate, ffa_w_up, ffa_w_down, ln_mix_w, w_in, conv_w, conv_b, dt_bias, a_log, d_skip, ssm_norm_w, hg_lb_raw, hg_norm_w, w_out, ln_ffb_w, ffb_w_gate, ffb_w_up, ffb_w_down, ln_f_w):
    depth = w_in.shape[0]
    nb, seq, _ = x_prompt.shape
    assert nb == N_STREAMS and x_sample.shape[:2] == (N_STREAMS, BLK) and seq % BLK == 0
    n_seq_blocks = seq // BLK
    n_prompt_blocks = n_seq_blocks + 1

    row = lambda a: a.astype(F32)[None, :]
    dt_cols = slice(SSM_WIDTH + CONV_CH, SSM_WIDTH + CONV_CH + SSM_WIDTH // SSM_HEADDIM)
    conv_out, ssm_out, hg_out = [], [], []
    for l in range(depth):
        w = w_in[l]
        params = {
            "ln_mix": row(ln_mix_w[l]),
            "w_in": jnp.concatenate([w[:, :SSM_WIDTH], w[:, dt_cols.stop:]], axis=1).astype(BF16),
            "w_x": w[:, SSM_WIDTH:dt_cols.start].astype(BF16),
            "w_dt": jnp.repeat(w[:, dt_cols], SSM_HEADDIM, axis=1).astype(BF16),
            "conv_w": conv_w[l].astype(F32),
            "conv_b": row(conv_b[l]),
            "dt_bias": _expand_heads(dt_bias[l]),
            "a_log": _expand_heads(a_log[l]),
            "d_skip": _expand_heads(d_skip[l]),
            "ssm_norm": row(ssm_norm_w[l]),
            "lb_raw": hg_lb_raw.astype(F32),
            "hg_norm": row(hg_norm_w[l]),
            "w_out": w_out[l].astype(BF16),
        }
        sconv = jnp.pad(state_conv[l].astype(F32), ((0, 0), (SUB - (CONV_WIDTH - 1), 0), (0, 0)))
        sssm = (state_ssm[l].astype(F32)
                .reshape(nb, SSM_GROUPS, SSM_GROUP_WIDTH // SSM_HEADDIM, SSM_HEADDIM, SSM_STATE)
                .transpose(0, 1, 4, 2, 3).reshape(nb, SSM_GROUPS, SSM_STATE, SSM_GROUP_WIDTH))
        shg = state_hgrn[l].astype(F32)

        ffa = (row(ln_ffa_w[l]), ffa_w_gate[l].astype(BF16), ffa_w_up[l].astype(BF16), ffa_w_down[l].astype(BF16))
        ffb = (row(ln_ffb_w[l]), ffb_w_gate[l].astype(BF16), ffb_w_up[l].astype(BF16), ffb_w_down[l].astype(BF16))
        if l == 0:
            h = _ffn_first(x_prompt.astype(F32), x_sample.astype(F32), meta_tokens.astype(F32), *ffa)
        else:
            h = _ffn(h, *ffa)
        h, oc, os_, og = _mixer(h, params, l, n_prompt_blocks, sconv, sssm, shg)
        if l == depth - 1:
            y_prompt, y_sample = _ffn_last(h, *ffb, row(ln_f_w))
        else:
            h = _ffn(h, *ffb)
        conv_out.append(oc[:, :, SUB - (CONV_WIDTH - 1):, :])
        ssm_out.append(os_.reshape(2, nb, SSM_GROUPS, SSM_STATE, SSM_GROUP_WIDTH // SSM_HEADDIM, SSM_HEADDIM)
                       .transpose(0, 1, 2, 4, 5, 3)
                       .reshape(2, nb, SSM_WIDTH // SSM_HEADDIM, SSM_HEADDIM, SSM_STATE))
        hg_out.append(og)

    conv_all, ssm_all, hg_all = jnp.stack(conv_out), jnp.stack(ssm_out), jnp.stack(hg_out)
    dt_ = x_prompt.dtype
    return (y_prompt.astype(dt_), y_sample.astype(x_sample.dtype),
            conv_all[:, 0].astype(dt_), ssm_all[:, 0].astype(dt_), hg_all[:, 0].astype(dt_),
            conv_all[:, 1].astype(state_conv.dtype), ssm_all[:, 1].astype(state_ssm.dtype),
            hg_all[:, 1].astype(state_hgrn.dtype))
```

```python
import functools

import numpy as np
import jax
import jax.numpy as jnp
from jax import lax
from jax.experimental import pallas as pl
from jax.experimental.pallas import tpu as pltpu

F32 = jnp.float32
BF16 = jnp.bfloat16

D_MODEL = 1024
N_STREAMS = 8
BLK = 64
ROWS = N_STREAMS * BLK
N_META = 16
N_PAD = BLK - N_META
SSM_WIDTH = 512
SSM_GROUPS = 2
SSM_GROUP_WIDTH = SSM_WIDTH // SSM_GROUPS
SSM_HEADDIM = 64
SSM_STATE = 128
CONV_CH = 1024
CONV_WIDTH = 4
HG_WIDTH = 512
HG_HEADS = 4
HG_DIM = 128
D_FF = 2816
MXU_TILE = 256
FF_CHUNK_EDGES = (0, 6 * MXU_TILE, D_FF)
EPS = 1e-6
LB_FLOOR = 1e-30
SUB = 8
LEVELS = (32, 16, 8)
LOG2E = 1.4426950408889634

N_PARTS = 2
STREAMS_PER_PART = N_STREAMS // N_PARTS
PART_ROWS = STREAMS_PER_PART * BLK
PROJ_CHUNK = 512
SLAB_SLOTS = 4
STAGGER = 3
LANES = 128
CONV_TILES = CONV_CH // LANES
BIG = float(2 ** 100)

C_Z, C_Q, C_F, C_I, C_G = 0, 512, 1024, 1536, 2048
IN_MAIN = 2560

VMEM_LIMIT_BYTES = 56 * 1024 * 1024


def _rms(x, w):
    return x * lax.rsqrt(jnp.mean(x * x, axis=-1, keepdims=True) + EPS) * w


def _silu(x):
    return x / (1.0 + jnp.exp(-x))


def _log1p(y):
    u = 1.0 + y
    return jnp.where(u == 1.0, y, jnp.log(u) * (y / (u - 1.0)))


def _softplus(x):
    return jnp.maximum(x, 0.0) + _log1p(jnp.exp(-jnp.abs(x)))


def _dot(a, b):
    return jnp.dot(a, b, preferred_element_type=F32)


def _dot_nt(a, b):
    return lax.dot_general(a, b, (((1,), (1,)), ((), ())), preferred_element_type=F32)


def _dot_tn(a, b):
    return lax.dot_general(a, b, (((0,), (0,)), ((), ())), preferred_element_type=F32)


def _embed_rows(piece, r0):
    s, w = piece.shape
    tile = 2 * SUB
    if s % tile:
        assert s == SUB and r0 % SUB == 0
        zeros = jnp.zeros((SUB, w), F32)
        piece = jnp.concatenate([zeros, piece] if r0 % tile else [piece, zeros], axis=0)
        r0, s = r0 - r0 % tile, tile
    assert r0 % tile == 0 and s % tile == 0
    parts = [piece.astype(BF16)]
    if r0:
        parts.insert(0, jnp.zeros((r0, w), BF16))
    if BLK - r0 - s:
        parts.append(jnp.zeros((BLK - r0 - s, w), BF16))
    return jnp.concatenate(parts, axis=0) if len(parts) > 1 else parts[0]


def _cumsum_rows(x, tri3):
    hi = x.astype(BF16)
    r1 = x - hi.astype(F32)
    mid = r1.astype(BF16)
    lo = (r1 - mid.astype(F32)).astype(BF16)
    return _dot(tri3, jnp.concatenate([hi, mid, lo], axis=0))


def _swiglu_step(h, lnw_ref, wg_ref, wu_ref, wd_ref):
    hn = _rms(h, lnw_ref[...]).astype(BF16)
    acc = jnp.zeros_like(h)
    for c0, c1 in zip(FF_CHUNK_EDGES[:-1], FF_CHUNK_EDGES[1:]):
        cs = slice(c0, c1)
        g = _dot(hn, wg_ref[:, cs])
        u = _dot(hn, wu_ref[:, cs])
        a = (_silu(g) * u).astype(BF16)
        acc = acc + _dot(a, wd_ref[cs, :])
    return h + 0.5 * acc


def _ffn_kernel(h_ref, lnw_ref, wg_ref, wu_ref, wd_ref, o_ref):
    o_ref[...] = _swiglu_step(h_ref[...], lnw_ref, wg_ref, wu_ref, wd_ref)


def _ffn_first_kernel(xp_ref, xs_ref, meta_ref, lnw_ref, wg_ref, wu_ref, wd_ref, o_ref):
    step = pl.program_id(0)
    first = jnp.concatenate([jnp.zeros((N_PAD, D_MODEL), F32), meta_ref[...]], axis=0)
    h = jnp.where(step == 0, first[None], xp_ref[...])
    h = jnp.where(step == pl.num_programs(0) - 1, xs_ref[...], h)
    o_ref[...] = _swiglu_step(h.reshape(ROWS, D_MODEL), lnw_ref, wg_ref, wu_ref, wd_ref)


def _ffn_last_kernel(h_ref, lnw_ref, wg_ref, wu_ref, wd_ref, lnf_ref, yp_ref, ys_ref):
    step = pl.program_id(0)
    last = pl.num_programs(0) - 1
    y = _rms(_swiglu_step(h_ref[...], lnw_ref, wg_ref, wu_ref, wd_ref), lnf_ref[...])
    y = y.reshape(N_STREAMS, BLK, D_MODEL)

    @pl.when(step < last)
    def _():
        yp_ref[...] = y

    @pl.when(step == last)
    def _():
        ys_ref[...] = y


def _const_spec(shape):
    nd = len(shape)
    return pl.BlockSpec(shape, lambda i: (0,) * nd, pipeline_mode=pl.Buffered(1))


_ROW_SPEC = pl.BlockSpec((ROWS, D_MODEL), lambda i: (i, 0))
_FFN_WEIGHT_SPECS = [_const_spec((1, D_MODEL)), _const_spec((D_MODEL, D_FF)),
                     _const_spec((D_MODEL, D_FF)), _const_spec((D_FF, D_MODEL))]
_FFN_PARAMS = pltpu.CompilerParams(dimension_semantics=("arbitrary",), vmem_limit_bytes=VMEM_LIMIT_BYTES)


def _prompt_block_spec(n_seq_blocks):
    return pl.BlockSpec((N_STREAMS, BLK, D_MODEL),
                        lambda i: (0, jnp.clip(i - 1, 0, n_seq_blocks - 1), 0))


def _ffn(h, lnw, wg, wu, wd):
    return pl.pallas_call(
        _ffn_kernel,
        grid=(h.shape[0] // ROWS,),
        in_specs=[_ROW_SPEC] + _FFN_WEIGHT_SPECS,
        out_specs=_ROW_SPEC,
        out_shape=jax.ShapeDtypeStruct(h.shape, F32),
        compiler_params=_FFN_PARAMS,
        name="ffn",
    )(h, lnw, wg, wu, wd)


def _ffn_first(x_prompt, x_sample, meta, lnw, wg, wu, wd):
    n_seq_blocks = x_prompt.shape[1] // BLK
    n_steps = n_seq_blocks + 2
    return pl.pallas_call(
        _ffn_first_kernel,
        grid=(n_steps,),
        in_specs=[_prompt_block_spec(n_seq_blocks), _const_spec((N_STREAMS, BLK, D_MODEL)),
                  _const_spec((N_META, D_MODEL))] + _FFN_WEIGHT_SPECS,
        out_specs=_ROW_SPEC,
        out_shape=jax.ShapeDtypeStruct((n_steps * ROWS, D_MODEL), F32),
        compiler_params=_FFN_PARAMS,
        name="ffn_first",
    )(x_prompt, x_sample, meta, lnw, wg, wu, wd)


def _ffn_last(h, lnw, wg, wu, wd, lnf):
    n_steps = h.shape[0] // ROWS
    n_seq_blocks = n_steps - 2
    return pl.pallas_call(
        _ffn_last_kernel,
        grid=(n_steps,),
        in_specs=[_ROW_SPEC] + _FFN_WEIGHT_SPECS + [_const_spec((1, D_MODEL))],
        out_specs=[_prompt_block_spec(n_seq_blocks),
                   pl.BlockSpec((N_STREAMS, BLK, D_MODEL), lambda i: (0, 0, 0))],
        out_shape=[jax.ShapeDtypeStruct((N_STREAMS, n_seq_blocks * BLK, D_MODEL), F32),
                   jax.ShapeDtypeStruct((N_STREAMS, BLK, D_MODEL), F32)],
        compiler_params=_FFN_PARAMS,
        name="ffn_last",
    )(h, lnw, wg, wu, wd, lnf)


def _mixer_kernel(h_ref, lnw_ref, win_ref, wx_ref, wdt_ref, convw_ref, convb_ref, dtb_ref, alog_ref,
                  dskip_ref, ssmnw_ref, lbraw_ref, hgnw_ref, wout_ref, hsum_ref, bmat_ref,
                  sconv_ref, sssm_ref, shg_ref,
                  hout_ref, oconv_ref, ossm_ref, ohg_ref,
                  proj_ref, xbc_ref, dtraw_ref, mix_ref, pbuf_ref, gb_ref, *, layer, n_prompt_blocks):
    step = pl.program_id(0)

    @pl.when(step == 0)
    def _():
        oconv_ref[...] = jnp.zeros(oconv_ref.shape, F32)
        ossm_ref[...] = jnp.zeros(ossm_ref.shape, F32)
        ohg_ref[...] = jnp.zeros(ohg_ref.shape, F32)

    @pl.when(step == n_prompt_blocks)
    def _():
        oconv_ref[0] = sconv_ref[...]
        ossm_ref[0] = sssm_ref[...]
        ohg_ref[0] = shg_ref[...]

    def in_projection(part):
        r0 = part * PART_ROWS
        rs = slice(r0, r0 + PART_ROWS)
        hn = _rms(h_ref[rs, :], lnw_ref[...]).astype(BF16)
        for c0 in range(0, IN_MAIN, PROJ_CHUNK):
            proj_ref[rs, c0:c0 + PROJ_CHUNK] = _dot(hn, win_ref[:, c0:c0 + PROJ_CHUNK])
            yield
        dtraw_ref[rs, :] = _dot(hn, wdt_ref[...])
        yield
        for c0 in range(0, CONV_CH, PROJ_CHUNK):
            x_new = _dot(hn, wx_ref[:, c0:c0 + PROJ_CHUNK])
            for k in range(STREAMS_PER_PART):
                for lt in range(PROJ_CHUNK // LANES):
                    xbc_ref[part * STREAMS_PER_PART + k, c0 // LANES + lt, SUB:SUB + BLK, :] = (
                        x_new[k * BLK:(k + 1) * BLK, lt * LANES:(lt + 1) * LANES])
            yield

    def out_projection(part):
        r0 = part * PART_ROWS
        rs = slice(r0, r0 + PART_ROWS)
        for c0 in range(0, D_MODEL, PROJ_CHUNK):
            cs = slice(c0, c0 + PROJ_CHUNK)
            hout_ref[rs, cs] = h_ref[rs, cs] + _dot(mix_ref[rs, :], wout_ref[:, cs])
            yield

    first_valid = jnp.where(step == 0, N_PAD, 0)
    t_col = lax.broadcasted_iota(jnp.int32, (BLK, 1), 0)
    valid = t_col >= first_valid

    tri3 = (lax.broadcasted_iota(jnp.int32, (BLK, 3 * BLK), 0)
            >= lax.broadcasted_iota(jnp.int32, (BLK, 3 * BLK), 1) % BLK).astype(BF16)

    conv_w = convw_ref[...]
    conv_b = convb_ref[...]
    a2 = -jnp.exp(alog_ref[...]) * LOG2E
    row_w = lax.broadcasted_iota(jnp.int32, (BLK, SSM_WIDTH), 0)
    col_w = lax.broadcasted_iota(jnp.int32, (BLK, SSM_WIDTH), 1)
    diag_pick = (col_w % BLK) == row_w
    row_g = lax.broadcasted_iota(jnp.int32, (BLK, SSM_GROUP_WIDTH), 0)
    col_g = lax.broadcasted_iota(jnp.int32, (BLK, SSM_GROUP_WIDTH), 1)
    causal_g = (col_g % BLK) <= row_g
    bd_r = lax.broadcasted_iota(jnp.int32, (SSM_GROUP_WIDTH, SSM_GROUP_WIDTH), 0)
    bd_c = lax.broadcasted_iota(jnp.int32, (SSM_GROUP_WIDTH, SSM_GROUP_WIDTH), 1)
    head_diag = ((bd_r // BLK) == (bd_c // BLK)).astype(BF16)

    raw = lbraw_ref[...]
    e_raw = jnp.exp(raw - jnp.max(raw, axis=0, keepdims=True))
    sm = e_raw / jnp.sum(e_raw, axis=0, keepdims=True)
    lb = jnp.clip(jnp.sum(sm[0:layer + 1], axis=0, keepdims=True) - sm[0:1], 0.0, 1.0 - 1e-6)
    log_lb = jnp.log(jnp.maximum(lb, LB_FLOOR))
    log_1m_lb = jnp.log1p(-lb)
    big_rows = jnp.where(lax.broadcasted_iota(jnp.int32, (BLK, HG_WIDTH), 0) == 0, BIG, 0.0).astype(BF16)
    di = lax.broadcasted_iota(jnp.int32, (BLK, HG_HEADS * BLK), 0)
    dc = lax.broadcasted_iota(jnp.int32, (BLK, HG_HEADS * BLK), 1) % BLK
    slab_masks = [((dc % SUB) == j) & ((dc // SUB) == (di // SUB)) for j in range(SUB)]

    def per_stream(b, slot):
        r0 = b * BLK if isinstance(b, int) else pl.multiple_of(b * BLK, BLK)
        rows = pl.ds(r0, BLK)

        tail = oconv_ref[0, b]
        conv_parts = []
        for lt in range(CONV_TILES):
            ls = slice(lt * LANES, (lt + 1) * LANES)
            xbc_ref[b, lt, 0:SUB, :] = tail[:, ls]
            part = conv_b[:, ls]
            for s in range(CONV_WIDTH - 1, -1, -1):
                tap = xbc_ref[b, lt, pl.ds(SUB - s, BLK, stride=1), :]
                part = part + tap * conv_w[CONV_WIDTH - 1 - s:CONV_WIDTH - s, ls]
            conv_parts.append(part)
        oconv_ref[0, b] = jnp.concatenate(
            [xbc_ref[b, lt, BLK:BLK + SUB, :] for lt in range(CONV_TILES)], axis=1)
        xbc = _silu(jnp.concatenate(conv_parts, axis=1))
        xs = xbc[:, 0:SSM_WIDTH]
        yield

        dt = _softplus(dtraw_ref[rows, :] + dtb_ref[...])
        dt = jnp.where(valid, dt, 0.0)
        cum = _cumsum_rows(dt * a2, tri3)
        cum_row = jnp.sum(jnp.where(diag_pick, cum, 0.0), axis=0, keepdims=True)
        yield
        y_parts = []
        for g in range(SSM_GROUPS):
            cs = slice(g * SSM_GROUP_WIDTH, (g + 1) * SSM_GROUP_WIDTH)
            b_g = xbc[:, SSM_WIDTH + g * SSM_STATE:SSM_WIDTH + (g + 1) * SSM_STATE].astype(BF16)
            c_g = xbc[:, SSM_WIDTH + (SSM_GROUPS + g) * SSM_STATE:
                      SSM_WIDTH + (SSM_GROUPS + g + 1) * SSM_STATE].astype(BF16)
            cum_g = cum[:, cs]
            cbm = _dot_nt(c_g, jnp.concatenate([b_g] * 4, axis=0))
            decay = jnp.where(causal_g, jnp.exp2(cum_g - cum_row[:, cs]), 0.0)
            xs_g = xs[:, cs]
            xdt = xs_g * dt[:, cs]
            xdt_bd = jnp.concatenate([xdt.astype(BF16)] * 4, axis=0) * head_diag
            st = ossm_ref[0, b, g]
            y_g = _dot((cbm * decay).astype(BF16), xdt_bd)
            y_g = y_g + _dot(c_g, st.astype(BF16)) * jnp.exp2(cum_g)
            y_g = y_g + dskip_ref[:, cs] * xs_g
            last = cum_g[BLK - 1:BLK]
            xdt_w = (xdt * jnp.exp2(last - cum_g)).astype(BF16)
            ossm_ref[0, b, g] = jnp.exp2(last) * st + _dot_tn(b_g, xdt_w)
            y_g = y_g * _silu(proj_ref[rows, C_Z + g * SSM_GROUP_WIDTH:C_Z + (g + 1) * SSM_GROUP_WIDTH])
            y_g = y_g * lax.rsqrt(jnp.mean(y_g * y_g, axis=-1, keepdims=True) + EPS)
            y_parts.append(y_g * ssmnw_ref[:, cs])
            yield
        mix_ref[rows, 0:SSM_WIDTH] = jnp.concatenate(y_parts, axis=1).astype(BF16)

        q = _silu(proj_ref[rows, C_Q:C_Q + HG_WIDTH])
        fr = proj_ref[rows, C_F:C_F + HG_WIDTH]
        log_sig = jnp.minimum(fr, 0.0) - jnp.log(1.0 + jnp.exp(-jnp.abs(fr)))
        t2 = log_1m_lb + log_sig
        lf = jnp.maximum(log_lb, t2) + jnp.log(1.0 + jnp.exp(-jnp.abs(log_lb - t2)))
        lf = jnp.where(valid, lf, 0.0)
        log2_key = jnp.where(valid, (t2 - fr) * LOG2E, -BIG)
        v16 = proj_ref[rows, C_I:C_I + HG_WIDTH].astype(BF16)
        cumh = _cumsum_rows(lf * LOG2E, tri3)
        g = cumh - log2_key
        lasth = cumh[BLK - 1:BLK]
        q_in = (q * jnp.exp2(cumh)).astype(BF16)
        k_out = jnp.exp2(lasth - g).astype(BF16)
        yield

        q_cols, k_cols = [], []
        for s in LEVELS:
            for m in range(0, BLK, 2 * s):
                lo, up = slice(m, m + s), slice(m + s, m + 2 * s)
                mid = cumh[m + s - 1:m + s]
                k_cols.append(_embed_rows(jnp.exp2(mid - g[lo]), m))
                q_cols.append(_embed_rows(q[up] * jnp.exp2(cumh[up] - mid), m + s))
        att_off = []
        for hh in range(HG_HEADS):
            hs = slice(hh * HG_DIM, (hh + 1) * HG_DIM)
            att_off.append(_dot_nt(jnp.concatenate([c[:, hs] for c in q_cols], axis=1),
                                   jnp.concatenate([c[:, hs] for c in k_cols], axis=1)))
        yield
        g_hi = g.astype(BF16)
        g_r1 = g - g_hi.astype(F32)
        g_mid = g_r1.astype(BF16)
        g_lo = (g_r1 - g_mid.astype(F32)).astype(BF16)
        gb_ref[slot] = _dot(bmat_ref[...], jnp.concatenate([g_hi, g_mid, g_lo, big_rows], axis=0))
        yield
        for sc in range(0, BLK // SUB, 2):
            for j in range(SUB):
                slabs = []
                for u in (sc, sc + 1):
                    rs = slice(u * SUB, (u + 1) * SUB)
                    g_j = gb_ref[slot, u * BLK + j * SUB:u * BLK + (j + 1) * SUB, :]
                    slabs.append(q[rs] * jnp.exp2(cumh[rs] - g_j))
                pbuf_ref[slot, j * BLK + sc * SUB:j * BLK + (sc + 2) * SUB, :] = (
                    jnp.concatenate(slabs, axis=0).astype(BF16))
            yield
        slab_sums = _dot(pbuf_ref[slot], hsum_ref[...])
        yield
        att_d = jnp.zeros((BLK, HG_HEADS * BLK), F32)
        for j in range(SUB):
            att_d = jnp.where(slab_masks[j], slab_sums[j * BLK:(j + 1) * BLK], att_d)
        decay_rows = jnp.exp2(lasth)
        o_parts = []
        for hh in range(HG_HEADS):
            hs = slice(hh * HG_DIM, (hh + 1) * HG_DIM)
            a_h = (att_off[hh] + att_d[:, hh * BLK:(hh + 1) * BLK]).astype(BF16)
            st = ohg_ref[0, b, hh]
            o_parts.append(_dot(jnp.concatenate([q_in[:, hs], a_h], axis=1),
                                jnp.concatenate([st.astype(BF16), v16[:, hs]], axis=0)))
            decay_cols = jnp.broadcast_to(decay_rows[:, hs], (HG_DIM, HG_DIM)).T
            ohg_ref[0, b, hh] = st * decay_cols + _dot_tn(k_out[:, hs], v16[:, hs])
            yield
        o_parts = [o_h * lax.rsqrt(jnp.mean(o_h * o_h, axis=-1, keepdims=True) + EPS) for o_h in o_parts]
        o = jnp.concatenate(o_parts, axis=1) * hgnw_ref[...]
        o = o * _silu(proj_ref[rows, C_G:C_G + HG_WIDTH])
        mix_ref[rows, SSM_WIDTH:SSM_WIDTH + HG_WIDTH] = o.astype(BF16)

    tasks = {}
    for part in range(N_PARTS):
        tasks["in%d" % part] = (in_projection(part), [] if part == 0 else ["in%d" % (part - 1)])
        first = part * STREAMS_PER_PART
        for b in range(first, first + STREAMS_PER_PART):
            tasks["s%d" % b] = (per_stream(b, b % SLAB_SLOTS), ["in%d" % part])
        tasks["out%d" % part] = (out_projection(part), ["s%d" % b for b in range(first, first + STREAMS_PER_PART)])
    done, active, rnd, last_stream_start = set(), [], 0, -STAGGER
    pending = list(tasks)
    while pending or active:
        for name in list(pending):
            is_stream = name.startswith("s")
            if all(dep in done for dep in tasks[name][1]) and not (is_stream and rnd < last_stream_start + STAGGER):
                pending.remove(name)
                active.append(name)
                if is_stream:
                    last_stream_start = rnd
        for name in list(active):
            if next(tasks[name][0], True) is not None:
                active.remove(name)
                done.add(name)
        rnd += 1


def _head_sum_matrix():
    head_r = np.arange(HG_WIDTH) // HG_DIM
    head_c = np.arange(HG_HEADS * BLK) // BLK
    return jnp.asarray(head_r[:, None] == head_c[None, :], dtype=BF16)


def _row_broadcast_matrix():
    r = np.arange(SUB * BLK)
    u, j, i = r // BLK, (r % BLK) // SUB, r % SUB
    c = np.arange(4 * BLK)
    part, t = c // BLK, c % BLK
    pick = (part[None, :] < 3) & (t[None, :] == (u * SUB + j)[:, None])
    mask = (part[None, :] == 3) & (t[None, :] == 0) & (i < j)[:, None]
    return jnp.asarray(pick | mask, dtype=BF16)


def _mixer(h, p, layer, n_prompt_blocks, sconv, sssm, shg):
    n_steps = h.shape[0] // ROWS
    row_spec = pl.BlockSpec((ROWS, D_MODEL), lambda i: (i, 0))
    group = lambda i: i // n_prompt_blocks
    conv_shape = (2, N_STREAMS, SUB, CONV_CH)
    ssm_shape = (2, N_STREAMS, SSM_GROUPS, SSM_STATE, SSM_GROUP_WIDTH)
    hg_shape = (2, N_STREAMS, HG_HEADS, HG_DIM, HG_DIM)
    in_specs = [
        row_spec,
        _const_spec((1, D_MODEL)),
        _const_spec((D_MODEL, IN_MAIN)),
        _const_spec((D_MODEL, CONV_CH)),
        _const_spec((D_MODEL, SSM_WIDTH)),
        _const_spec((CONV_WIDTH, CONV_CH)),
        _const_spec((1, CONV_CH)),
        _const_spec((1, SSM_WIDTH)),
        _const_spec((1, SSM_WIDTH)),
        _const_spec((1, SSM_WIDTH)),
        _const_spec((1, SSM_WIDTH)),
        _const_spec((p["lb_raw"].shape[0], HG_WIDTH)),
        _const_spec((1, HG_WIDTH)),
        _const_spec((D_MODEL, D_MODEL)),
        _const_spec((HG_WIDTH, HG_HEADS * BLK)),
        _const_spec((SUB * BLK, 4 * BLK)),
        _const_spec(conv_shape[1:]),
        _const_spec(ssm_shape[1:]),
        _const_spec(hg_shape[1:]),
    ]
    out_specs = [
        row_spec,
        pl.BlockSpec((1,) + conv_shape[1:], lambda i: (group(i), 0, 0, 0)),
        pl.BlockSpec((1,) + ssm_shape[1:], lambda i: (group(i), 0, 0, 0, 0)),
        pl.BlockSpec((1,) + hg_shape[1:], lambda i: (group(i), 0, 0, 0, 0)),
    ]
    out_shape = [
        jax.ShapeDtypeStruct(h.shape, F32),
        jax.ShapeDtypeStruct(conv_shape, F32),
        jax.ShapeDtypeStruct(ssm_shape, F32),
        jax.ShapeDtypeStruct(hg_shape, F32),
    ]
    return pl.pallas_call(
        functools.partial(_mixer_kernel, layer=layer, n_prompt_blocks=n_prompt_blocks),
        grid=(n_steps,),
        in_specs=in_specs,
        out_specs=out_specs,
        out_shape=out_shape,
        scratch_shapes=[
            pltpu.VMEM((ROWS, IN_MAIN), F32),
            pltpu.VMEM((N_STREAMS, CONV_TILES, SUB + BLK, LANES), F32),
            pltpu.VMEM((ROWS, SSM_WIDTH), F32),
            pltpu.VMEM((ROWS, D_MODEL), BF16),
            pltpu.VMEM((SLAB_SLOTS, SUB * BLK, HG_WIDTH), BF16),
            pltpu.VMEM((SLAB_SLOTS, SUB * BLK, HG_WIDTH), F32),
        ],
        compiler_params=pltpu.CompilerParams(
            dimension_semantics=("arbitrary",), vmem_limit_bytes=VMEM_LIMIT_BYTES),
        name="mixer",
    )(h, p["ln_mix"], p["w_in"], p["w_x"], p["w_dt"], p["conv_w"], p["conv_b"], p["dt_bias"], p["a_log"],
      p["d_skip"], p["ssm_norm"], p["lb_raw"], p["hg_norm"], p["w_out"], _head_sum_matrix(),
      _row_broadcast_matrix(), sconv, sssm, shg)


def _expand_heads(x):
    return jnp.repeat(x.astype(F32), SSM_HEADDIM)[None, :]


def kernel(x_prompt, x_sample, state_conv, state_ssm, state_hgrn, meta_tokens, ln_ffa_w, ffa_w_gate, ffa_w_up, ffa_w_down, ln_mix_w, w_in, conv_w, conv_b, dt_bias, a_log, d_skip, ssm_norm_w, hg_lb_raw, hg_norm_w, w_out, ln_ffb_w, ffb_w_gate, ffb_w_up, ffb_w_down, ln_f_w):
    depth = w_in.shape[0]
    nb, seq, _ = x_prompt.shape
    assert nb == N_STREAMS and x_sample.shape[:2] == (N_STREAMS, BLK) and seq % BLK == 0
    n_seq_blocks = seq // BLK
    n_prompt_blocks = n_seq_blocks + 1

    row = lambda a: a.astype(F32)[None, :]
    dt_cols = slice(SSM_WIDTH + CONV_CH, SSM_WIDTH + CONV_CH + SSM_WIDTH // SSM_HEADDIM)
    conv_out, ssm_out, hg_out = [], [], []
    for l in range(depth):
        w = w_in[l]
        params = {
            "ln_mix": row(ln_mix_w[l]),
            "w_in": jnp.concatenate([w[:, :SSM_WIDTH], w[:, dt_cols.stop:]], axis=1).astype(BF16),
            "w_x": w[:, SSM_WIDTH:dt_cols.start].astype(BF16),
            "w_dt": jnp.repeat(w[:, dt_cols], SSM_HEADDIM, axis=1).astype(BF16),
            "conv_w": conv_w[l].astype(F32),
            "conv_b": row(conv_b[l]),
            "dt_bias": _expand_heads(dt_bias[l]),
            "a_log": _expand_heads(a_log[l]),
            "d_skip": _expand_heads(d_skip[l]),
            "ssm_norm": row(ssm_norm_w[l]),
            "lb_raw": hg_lb_raw.astype(F32),
            "hg_norm": row(hg_norm_w[l]),
            "w_out": w_out[l].astype(BF16),
        }
        sconv = jnp.pad(state_conv[l].astype(F32), ((0, 0), (SUB - (CONV_WIDTH - 1), 0), (0, 0)))
        sssm = (state_ssm[l].astype(F32)
                .reshape(nb, SSM_GROUPS, SSM_GROUP_WIDTH // SSM_HEADDIM, SSM_HEADDIM, SSM_STATE)
                .transpose(0, 1, 4, 2, 3).reshape(nb, SSM_GROUPS, SSM_STATE, SSM_GROUP_WIDTH))
        shg = state_hgrn[l].astype(F32)

        ffa = (row(ln_ffa_w[l]), ffa_w_gate[l].astype(BF16), ffa_w_up[l].astype(BF16), ffa_w_down[l].astype(BF16))
        ffb = (row(ln_ffb_w[l]), ffb_w_gate[l].astype(BF16), ffb_w_up[l].astype(BF16), ffb_w_down[l].astype(BF16))
        if l == 0:
            h = _ffn_first(x_prompt.astype(F32), x_sample.astype(F32), meta_tokens.astype(F32), *ffa)
        else:
            h = _ffn(h, *ffa)
        h, oc, os_, og = _mixer(h, params, l, n_prompt_blocks, sconv, sssm, shg)
        if l == depth - 1:
            y_prompt, y_sample = _ffn_last(h, *ffb, row(ln_f_w))
        else:
            h = _ffn(h, *ffb)
        conv_out.append(oc[:, :, SUB - (CONV_WIDTH - 1):, :])
        ssm_out.append(os_.reshape(2, nb, SSM_GROUPS, SSM_STATE, SSM_GROUP_WIDTH // SSM_HEADDIM, SSM_HEADDIM)
                       .transpose(0, 1, 2, 4, 5, 3)
                       .reshape(2, nb, SSM_WIDTH // SSM_HEADDIM, SSM_HEADDIM, SSM_STATE))
        hg_out.append(og)

    conv_all, ssm_all, hg_all = jnp.stack(conv_out), jnp.stack(ssm_out), jnp.stack(hg_out)
    dt_ = x_prompt.dtype
    return (y_prompt.astype(dt_), y_sample.astype(x_sample.dtype),
            conv_all[:, 0].astype(dt_), ssm_all[:, 0].astype(dt_), hg_all[:, 0].astype(dt_),
            conv_all[:, 1].astype(state_conv.dtype), ssm_all[:, 1].astype(state_ssm.dtype),
            hg_all[:, 1].astype(state_hgrn.dtype))
```

```python
import functools

import numpy as np
import jax
import jax.numpy as jnp
from jax import lax
from jax.experimental import pallas as pl
from jax.experimental.pallas import tpu as pltpu

F32 = jnp.float32
BF16 = jnp.bfloat16

D_MODEL = 1024
N_STREAMS = 8
BLK = 64
ROWS = N_STREAMS * BLK
N_META = 16
N_PAD = BLK - N_META
SSM_WIDTH = 512
SSM_GROUPS = 2
SSM_GROUP_WIDTH = SSM_WIDTH // SSM_GROUPS
SSM_HEADDIM = 64
SSM_STATE = 128
CONV_CH = 1024
CONV_WIDTH = 4
HG_WIDTH = 512
HG_HEADS = 4
HG_DIM = 128
D_FF = 2816
MXU_TILE = 256
FF_CHUNK_EDGES = (0, 6 * MXU_TILE, D_FF)
EPS = 1e-6
LB_FLOOR = 1e-30
SUB = 8
LEVELS = (32, 16, 8)
LOG2E = 1.4426950408889634

N_PARTS = 2
STREAMS_PER_PART = N_STREAMS // N_PARTS
PART_ROWS = STREAMS_PER_PART * BLK
PROJ_CHUNK = 256
SLAB_SLOTS = 4
STAGGER = (3, 3)
LANES = 128
CONV_TILES = CONV_CH // LANES
BIG = float(2 ** 100)

C_Z, C_Q, C_F, C_I, C_G = 0, 512, 1024, 1536, 2048
IN_MAIN = 2560

VMEM_LIMIT_BYTES = 56 * 1024 * 1024


def _rms(x, w):
    return x * lax.rsqrt(jnp.mean(x * x, axis=-1, keepdims=True) + EPS) * w


def _silu(x):
    return x / (1.0 + jnp.exp(-x))


def _log1p(y):
    u = 1.0 + y
    return jnp.where(u == 1.0, y, jnp.log(u) * (y / (u - 1.0)))


def _softplus(x):
    return jnp.maximum(x, 0.0) + _log1p(jnp.exp(-jnp.abs(x)))


def _dot(a, b):
    return jnp.dot(a, b, preferred_element_type=F32)


def _dot_nt(a, b):
    return lax.dot_general(a, b, (((1,), (1,)), ((), ())), preferred_element_type=F32)


def _dot_tn(a, b):
    return lax.dot_general(a, b, (((0,), (0,)), ((), ())), preferred_element_type=F32)


def _embed_rows(piece, r0):
    s, w = piece.shape
    tile = 2 * SUB
    if s % tile:
        assert s == SUB and r0 % SUB == 0
        zeros = jnp.zeros((SUB, w), F32)
        piece = jnp.concatenate([zeros, piece] if r0 % tile else [piece, zeros], axis=0)
        r0, s = r0 - r0 % tile, tile
    assert r0 % tile == 0 and s % tile == 0
    parts = [piece.astype(BF16)]
    if r0:
        parts.insert(0, jnp.zeros((r0, w), BF16))
    if BLK - r0 - s:
        parts.append(jnp.zeros((BLK - r0 - s, w), BF16))
    return jnp.concatenate(parts, axis=0) if len(parts) > 1 else parts[0]


def _split3(x):
    hi = x.astype(BF16)
    r1 = x - hi.astype(F32)
    mid = r1.astype(BF16)
    lo = (r1 - mid.astype(F32)).astype(BF16)
    return jnp.concatenate([hi, mid, lo], axis=0)


def _swiglu_step(h, lnw_ref, wg_ref, wu_ref, wd_ref):
    hn = _rms(h, lnw_ref[...]).astype(BF16)
    acc = jnp.zeros_like(h)
    for c0, c1 in zip(FF_CHUNK_EDGES[:-1], FF_CHUNK_EDGES[1:]):
        cs = slice(c0, c1)
        g = _dot(hn, wg_ref[:, cs])
        u = _dot(hn, wu_ref[:, cs])
        a = (_silu(g) * u).astype(BF16)
        acc = acc + _dot(a, wd_ref[cs, :])
    return h + 0.5 * acc


def _ffn_kernel(h_ref, lnw_ref, wg_ref, wu_ref, wd_ref, o_ref):
    o_ref[...] = _swiglu_step(h_ref[...], lnw_ref, wg_ref, wu_ref, wd_ref)


def _ffn_first_kernel(xp_ref, xs_ref, meta_ref, lnw_ref, wg_ref, wu_ref, wd_ref, o_ref):
    step = pl.program_id(0)
    first = jnp.concatenate([jnp.zeros((N_PAD, D_MODEL), F32), meta_ref[...]], axis=0)
    h = jnp.where(step == 0, first[None], xp_ref[...])
    h = jnp.where(step == pl.num_programs(0) - 1, xs_ref[...], h)
    o_ref[...] = _swiglu_step(h.reshape(ROWS, D_MODEL), lnw_ref, wg_ref, wu_ref, wd_ref)


def _ffn_last_kernel(h_ref, lnw_ref, wg_ref, wu_ref, wd_ref, lnf_ref, yp_ref, ys_ref):
    step = pl.program_id(0)
    last = pl.num_programs(0) - 1
    y = _rms(_swiglu_step(h_ref[...], lnw_ref, wg_ref, wu_ref, wd_ref), lnf_ref[...])
    y = y.reshape(N_STREAMS, BLK, D_MODEL)

    @pl.when(step < last)
    def _():
        yp_ref[...] = y

    @pl.when(step == last)
    def _():
        ys_ref[...] = y


def _const_spec(shape):
    nd = len(shape)
    return pl.BlockSpec(shape, lambda i: (0,) * nd, pipeline_mode=pl.Buffered(1))


_ROW_SPEC = pl.BlockSpec((ROWS, D_MODEL), lambda i: (i, 0))
_FFN_WEIGHT_SPECS = [_const_spec((1, D_MODEL)), _const_spec((D_MODEL, D_FF)),
                     _const_spec((D_MODEL, D_FF)), _const_spec((D_FF, D_MODEL))]
_FFN_PARAMS = pltpu.CompilerParams(dimension_semantics=("arbitrary",), vmem_limit_bytes=VMEM_LIMIT_BYTES)


def _prompt_block_spec(n_seq_blocks):
    return pl.BlockSpec((N_STREAMS, BLK, D_MODEL),
                        lambda i: (0, jnp.clip(i - 1, 0, n_seq_blocks - 1), 0))


def _ffn(h, lnw, wg, wu, wd):
    return pl.pallas_call(
        _ffn_kernel,
        grid=(h.shape[0] // ROWS,),
        in_specs=[_ROW_SPEC] + _FFN_WEIGHT_SPECS,
        out_specs=_ROW_SPEC,
        out_shape=jax.ShapeDtypeStruct(h.shape, F32),
        compiler_params=_FFN_PARAMS,
        name="ffn",
    )(h, lnw, wg, wu, wd)


def _ffn_first(x_prompt, x_sample, meta, lnw, wg, wu, wd):
    n_seq_blocks = x_prompt.shape[1] // BLK
    n_steps = n_seq_blocks + 2
    return pl.pallas_call(
        _ffn_first_kernel,
        grid=(n_steps,),
        in_specs=[_prompt_block_spec(n_seq_blocks), _const_spec((N_STREAMS, BLK, D_MODEL)),
                  _const_spec((N_META, D_MODEL))] + _FFN_WEIGHT_SPECS,
        out_specs=_ROW_SPEC,
        out_shape=jax.ShapeDtypeStruct((n_steps * ROWS, D_MODEL), F32),
        compiler_params=_FFN_PARAMS,
        name="ffn_first",
    )(x_prompt, x_sample, meta, lnw, wg, wu, wd)


def _ffn_last(h, lnw, wg, wu, wd, lnf):
    n_steps = h.shape[0] // ROWS
    n_seq_blocks = n_steps - 2
    return pl.pallas_call(
        _ffn_last_kernel,
        grid=(n_steps,),
        in_specs=[_ROW_SPEC] + _FFN_WEIGHT_SPECS + [_const_spec((1, D_MODEL))],
        out_specs=[_prompt_block_spec(n_seq_blocks),
                   pl.BlockSpec((N_STREAMS, BLK, D_MODEL), lambda i: (0, 0, 0))],
        out_shape=[jax.ShapeDtypeStruct((N_STREAMS, n_seq_blocks * BLK, D_MODEL), F32),
                   jax.ShapeDtypeStruct((N_STREAMS, BLK, D_MODEL), F32)],
        compiler_params=_FFN_PARAMS,
        name="ffn_last",
    )(h, lnw, wg, wu, wd, lnf)


def _mixer_kernel(h_ref, lnw_ref, win_ref, wx_ref, wdt_ref, convw_ref, convb_ref, dtb_ref, alog_ref,
                  dskip_ref, ssmnw_ref, lbraw_ref, hgnw_ref, wout_ref, bmat_ref,
                  sconv_ref, sssm_ref, shg_ref,
                  hout_ref, oconv_ref, ossm_ref, ohg_ref,
                  proj_ref, xbc_ref, dtraw_ref, mix_ref, gb_ref, *, layer, n_prompt_blocks):
    step = pl.program_id(0)

    @pl.when(step == 0)
    def _():
        oconv_ref[...] = jnp.zeros(oconv_ref.shape, F32)
        ossm_ref[...] = jnp.zeros(ossm_ref.shape, F32)
        ohg_ref[...] = jnp.zeros(ohg_ref.shape, F32)

    @pl.when(step == n_prompt_blocks)
    def _():
        oconv_ref[0] = sconv_ref[...]
        ossm_ref[0] = sssm_ref[...]
        ohg_ref[0] = shg_ref[...]

    ready = [set() for _ in range(N_PARTS)]

    def in_projection(part):
        r0 = part * PART_ROWS
        rs = slice(r0, r0 + PART_ROWS)
        hn = _rms(h_ref[rs, :], lnw_ref[...]).astype(BF16)
        yield
        for c0 in range(0, CONV_CH, PROJ_CHUNK):
            x_new = _dot(hn, wx_ref[:, c0:c0 + PROJ_CHUNK])
            for k in range(STREAMS_PER_PART):
                for lt in range(PROJ_CHUNK // LANES):
                    xbc_ref[part * STREAMS_PER_PART + k, c0 // LANES + lt, SUB:SUB + BLK, :] = (
                        x_new[k * BLK:(k + 1) * BLK, lt * LANES:(lt + 1) * LANES])
            yield
        ready[part].add("x")
        for c0 in range(0, SSM_WIDTH, PROJ_CHUNK):
            dtraw_ref[rs, c0:c0 + PROJ_CHUNK] = _dot(hn, wdt_ref[:, c0:c0 + PROJ_CHUNK])
            yield
        ready[part].add("dt")
        for name, c_lo in (("z", C_Z), ("q", C_Q), ("f", C_F), ("i", C_I), ("g", C_G)):
            for c0 in range(c_lo, c_lo + HG_WIDTH, PROJ_CHUNK):
                proj_ref[rs, c0:c0 + PROJ_CHUNK] = _dot(hn, win_ref[:, c0:c0 + PROJ_CHUNK])
                yield
            ready[part].add(name)

    def out_projection(part):
        r0 = part * PART_ROWS
        rs = slice(r0, r0 + PART_ROWS)
        for c0 in range(0, D_MODEL, PROJ_CHUNK):
            cs = slice(c0, c0 + PROJ_CHUNK)
            hout_ref[rs, cs] = h_ref[rs, cs] + _dot(mix_ref[rs, :], wout_ref[:, cs])
            yield

    first_valid = jnp.where(step == 0, N_PAD, 0)
    t_col = lax.broadcasted_iota(jnp.int32, (BLK, 1), 0)
    valid = t_col >= first_valid

    tri3 = (lax.broadcasted_iota(jnp.int32, (BLK, 3 * BLK), 0)
            >= lax.broadcasted_iota(jnp.int32, (BLK, 3 * BLK), 1) % BLK).astype(BF16)

    conv_w = convw_ref[...]
    conv_b = convb_ref[...]
    a2 = -jnp.exp(alog_ref[...]) * LOG2E
    row_w = lax.broadcasted_iota(jnp.int32, (BLK, SSM_WIDTH), 0)
    col_w = lax.broadcasted_iota(jnp.int32, (BLK, SSM_WIDTH), 1)
    diag_pick = (col_w % BLK) == row_w
    row_g = lax.broadcasted_iota(jnp.int32, (BLK, SSM_GROUP_WIDTH), 0)
    col_g = lax.broadcasted_iota(jnp.int32, (BLK, SSM_GROUP_WIDTH), 1)
    causal_g = (col_g % BLK) <= row_g
    bd_r = lax.broadcasted_iota(jnp.int32, (SSM_GROUP_WIDTH, SSM_GROUP_WIDTH), 0)
    bd_c = lax.broadcasted_iota(jnp.int32, (SSM_GROUP_WIDTH, SSM_GROUP_WIDTH), 1)
    head_diag = ((bd_r // BLK) == (bd_c // BLK)).astype(BF16)

    raw = lbraw_ref[...]
    e_raw = jnp.exp(raw - jnp.max(raw, axis=0, keepdims=True))
    sm = e_raw / jnp.sum(e_raw, axis=0, keepdims=True)
    lb = jnp.clip(jnp.sum(sm[0:layer + 1], axis=0, keepdims=True) - sm[0:1], 0.0, 1.0 - 1e-6)
    log_lb = jnp.log(jnp.maximum(lb, LB_FLOOR))
    log_1m_lb = jnp.log1p(-lb)
    big_rows = jnp.where(lax.broadcasted_iota(jnp.int32, (BLK, HG_WIDTH), 0) == 0, BIG, 0.0).astype(BF16)
    piece_col = lax.broadcasted_iota(jnp.int32, (SUB, BLK), 1)

    def per_stream(b, slot):
        r0 = b * BLK if isinstance(b, int) else pl.multiple_of(b * BLK, BLK)
        rows = pl.ds(r0, BLK)
        projected = ready[b // STREAMS_PER_PART]

        def wait_for(name):
            while name not in projected:
                yield

        yield from wait_for("x")
        tail = oconv_ref[0, b]
        conv_parts = []
        for lt in range(CONV_TILES):
            ls = slice(lt * LANES, (lt + 1) * LANES)
            xbc_ref[b, lt, 0:SUB, :] = tail[:, ls]
            part = conv_b[:, ls]
            for s in range(CONV_WIDTH - 1, -1, -1):
                tap = xbc_ref[b, lt, pl.ds(SUB - s, BLK, stride=1), :]
                part = part + tap * conv_w[CONV_WIDTH - 1 - s:CONV_WIDTH - s, ls]
            conv_parts.append(part)
        oconv_ref[0, b] = jnp.concatenate(
            [xbc_ref[b, lt, BLK:BLK + SUB, :] for lt in range(CONV_TILES)], axis=1)
        xbc = _silu(jnp.concatenate(conv_parts, axis=1))
        xs = xbc[:, 0:SSM_WIDTH]
        yield

        yield from wait_for("dt")
        dt = _softplus(dtraw_ref[rows, :] + dtb_ref[...])
        dt = jnp.where(valid, dt, 0.0)
        decay_pieces = _split3(dt * a2)
        groups = [slice(g * SSM_GROUP_WIDTH, (g + 1) * SSM_GROUP_WIDTH) for g in range(SSM_GROUPS)]
        b16 = [xbc[:, SSM_WIDTH + g * SSM_STATE:SSM_WIDTH + (g + 1) * SSM_STATE].astype(BF16)
               for g in range(SSM_GROUPS)]
        c16 = [xbc[:, SSM_WIDTH + (SSM_GROUPS + g) * SSM_STATE:
                   SSM_WIDTH + (SSM_GROUPS + g + 1) * SSM_STATE].astype(BF16) for g in range(SSM_GROUPS)]
        xdt = [xs[:, cs] * dt[:, cs] for cs in groups]
        xdt_bd = [jnp.concatenate([x.astype(BF16)] * 4, axis=0) * head_diag for x in xdt]
        st16 = [ossm_ref[0, b, g].astype(BF16) for g in range(SSM_GROUPS)]
        yield
        cum = _dot(tri3, decay_pieces)
        cbm = [_dot_nt(c16[g], jnp.concatenate([b16[g]] * 4, axis=0)) for g in range(SSM_GROUPS)]
        y_state = [_dot(c16[g], st16[g]) for g in range(SSM_GROUPS)]
        yield
        cum_row = jnp.sum(jnp.where(diag_pick, cum, 0.0), axis=0, keepdims=True)
        att16, xdt_w, last = [], [], []
        for g, cs in enumerate(groups):
            decay = jnp.where(causal_g, jnp.exp2(cum[:, cs] - cum_row[:, cs]), 0.0)
            att16.append((cbm[g] * decay).astype(BF16))
            last.append(cum[BLK - 1:BLK, cs])
            xdt_w.append((xdt[g] * jnp.exp2(last[g] - cum[:, cs])).astype(BF16))
        yield
        y_intra = [_dot(att16[g], xdt_bd[g]) for g in range(SSM_GROUPS)]
        st_new = [_dot_tn(b16[g], xdt_w[g]) for g in range(SSM_GROUPS)]
        yield
        yield from wait_for("z")
        y_parts = []
        for g, cs in enumerate(groups):
            ossm_ref[0, b, g] = jnp.exp2(last[g]) * ossm_ref[0, b, g] + st_new[g]
            y_g = y_intra[g] + y_state[g] * jnp.exp2(cum[:, cs]) + dskip_ref[:, cs] * xs[:, cs]
            y_g = y_g * _silu(proj_ref[rows, C_Z + g * SSM_GROUP_WIDTH:C_Z + (g + 1) * SSM_GROUP_WIDTH])
            y_g = y_g * lax.rsqrt(jnp.mean(y_g * y_g, axis=-1, keepdims=True) + EPS)
            y_parts.append(y_g * ssmnw_ref[:, cs])
        mix_ref[rows, 0:SSM_WIDTH] = jnp.concatenate(y_parts, axis=1).astype(BF16)
        yield

        yield from wait_for("f")
        q = _silu(proj_ref[rows, C_Q:C_Q + HG_WIDTH])
        fr = proj_ref[rows, C_F:C_F + HG_WIDTH]
        log_sig = jnp.minimum(fr, 0.0) - jnp.log(1.0 + jnp.exp(-jnp.abs(fr)))
        t2 = log_1m_lb + log_sig
        lf = jnp.maximum(log_lb, t2) + jnp.log(1.0 + jnp.exp(-jnp.abs(log_lb - t2)))
        lf = jnp.where(valid, lf, 0.0)
        log2_key = jnp.where(valid, (t2 - fr) * LOG2E, -BIG)
        forget_pieces = _split3(lf * LOG2E)
        yield
        cumh = _dot(tri3, forget_pieces)
        yield
        g = cumh - log2_key
        lasth = cumh[BLK - 1:BLK]
        q_in = (q * jnp.exp2(cumh)).astype(BF16)
        k_out = jnp.exp2(lasth - g).astype(BF16)
        g_pieces = jnp.concatenate([_split3(g), big_rows], axis=0)

        q_cols, k_cols = [], []
        for s in LEVELS:
            for m in range(0, BLK, 2 * s):
                lo, up = slice(m, m + s), slice(m + s, m + 2 * s)
                mid = cumh[m + s - 1:m + s]
                k_cols.append(_embed_rows(jnp.exp2(mid - g[lo]), m))
                q_cols.append(_embed_rows(q[up] * jnp.exp2(cumh[up] - mid), m + s))
        yield
        gb_ref[slot] = _dot(bmat_ref[...], g_pieces)
        att_off = []
        for hh in range(HG_HEADS):
            hs = slice(hh * HG_DIM, (hh + 1) * HG_DIM)
            att_off.append(_dot_nt(jnp.concatenate([c[:, hs] for c in q_cols], axis=1),
                                   jnp.concatenate([c[:, hs] for c in k_cols], axis=1)))
        yield
        diag_pieces = []
        for u in range(BLK // SUB):
            rs = slice(u * SUB, (u + 1) * SUB)
            pieces = [jnp.zeros((SUB, BLK), F32) for _ in range(HG_HEADS)]
            for j in range(SUB):
                g_j = gb_ref[slot, u * BLK + j * SUB:u * BLK + (j + 1) * SUB, :]
                slab = q[rs] * jnp.exp2(cumh[rs] - g_j)
                for hh in range(HG_HEADS):
                    pair_sum = jnp.sum(slab[:, hh * HG_DIM:(hh + 1) * HG_DIM], axis=-1, keepdims=True)
                    pieces[hh] = jnp.where(piece_col == u * SUB + j, pair_sum, pieces[hh])
            diag_pieces.append(pieces)
            if u % 2:
                yield
        yield from wait_for("i")
        v16 = proj_ref[rows, C_I:C_I + HG_WIDTH].astype(BF16)
        decay_rows = jnp.exp2(lasth)
        heads = [slice(hh * HG_DIM, (hh + 1) * HG_DIM) for hh in range(HG_HEADS)]
        lhs, rhs, decay_cols = [], [], []
        for hh, hs in enumerate(heads):
            att_d = jnp.concatenate([diag_pieces[u][hh] for u in range(BLK // SUB)], axis=0)
            a_h = (att_off[hh] + att_d).astype(BF16)
            lhs.append(jnp.concatenate([q_in[:, hs], a_h], axis=1))
            rhs.append(jnp.concatenate([ohg_ref[0, b, hh].astype(BF16), v16[:, hs]], axis=0))
            decay_cols.append(jnp.broadcast_to(decay_rows[:, hs], (HG_DIM, HG_DIM)).T)
        yield
        o_parts = [_dot(lhs[hh], rhs[hh]) for hh in range(HG_HEADS)]
        kv_new = [_dot_tn(k_out[:, hs], v16[:, hs]) for hs in heads]
        yield
        for hh in range(HG_HEADS):
            ohg_ref[0, b, hh] = ohg_ref[0, b, hh] * decay_cols[hh] + kv_new[hh]
        yield from wait_for("g")
        o_parts = [o_h * lax.rsqrt(jnp.mean(o_h * o_h, axis=-1, keepdims=True) + EPS) for o_h in o_parts]
        o = jnp.concatenate(o_parts, axis=1) * hgnw_ref[...]
        o = o * _silu(proj_ref[rows, C_G:C_G + HG_WIDTH])
        mix_ref[rows, SSM_WIDTH:SSM_WIDTH + HG_WIDTH] = o.astype(BF16)

    tasks = {}
    for part in range(N_PARTS):
        first = part * STREAMS_PER_PART
        streams = ["s%d" % b for b in range(first, first + STREAMS_PER_PART)]
        tasks["in%d" % part] = (in_projection(part), [] if part == 0 else ["in%d" % (part - 1)])
        for b in range(first, first + STREAMS_PER_PART):
            tasks["s%d" % b] = (per_stream(b, b % SLAB_SLOTS), [])
        tasks["out%d" % part] = (out_projection(part), streams)
    done, active, rnd, next_stream_start = set(), [], 0, 0
    pending = list(tasks)
    while pending or active:
        for name in list(pending):
            is_stream = name.startswith("s")
            if all(dep in done for dep in tasks[name][1]) and not (is_stream and rnd < next_stream_start):
                pending.remove(name)
                active.append(name)
                if is_stream:
                    next_stream_start = rnd + STAGGER[int(name[1:]) // STREAMS_PER_PART]
        for name in list(active):
            if next(tasks[name][0], True) is not None:
                active.remove(name)
                done.add(name)
        rnd += 1


def _row_broadcast_matrix():
    r = np.arange(SUB * BLK)
    u, j, i = r // BLK, (r % BLK) // SUB, r % SUB
    c = np.arange(4 * BLK)
    part, t = c // BLK, c % BLK
    pick = (part[None, :] < 3) & (t[None, :] == (u * SUB + j)[:, None])
    mask = (part[None, :] == 3) & (t[None, :] == 0) & (i < j)[:, None]
    return jnp.asarray(pick | mask, dtype=BF16)


def _mixer(h, p, layer, n_prompt_blocks, sconv, sssm, shg):
    n_steps = h.shape[0] // ROWS
    row_spec = pl.BlockSpec((ROWS, D_MODEL), lambda i: (i, 0))
    group = lambda i: i // n_prompt_blocks
    conv_shape = (2, N_STREAMS, SUB, CONV_CH)
    ssm_shape = (2, N_STREAMS, SSM_GROUPS, SSM_STATE, SSM_GROUP_WIDTH)
    hg_shape = (2, N_STREAMS, HG_HEADS, HG_DIM, HG_DIM)
    in_specs = [
        row_spec,
        _const_spec((1, D_MODEL)),
        _const_spec((D_MODEL, IN_MAIN)),
        _const_spec((D_MODEL, CONV_CH)),
        _const_spec((D_MODEL, SSM_WIDTH)),
        _const_spec((CONV_WIDTH, CONV_CH)),
        _const_spec((1, CONV_CH)),
        _const_spec((1, SSM_WIDTH)),
        _const_spec((1, SSM_WIDTH)),
        _const_spec((1, SSM_WIDTH)),
        _const_spec((1, SSM_WIDTH)),
        _const_spec((p["lb_raw"].shape[0], HG_WIDTH)),
        _const_spec((1, HG_WIDTH)),
        _const_spec((D_MODEL, D_MODEL)),
        _const_spec((SUB * BLK, 4 * BLK)),
        _const_spec(conv_shape[1:]),
        _const_spec(ssm_shape[1:]),
        _const_spec(hg_shape[1:]),
    ]
    out_specs = [
        row_spec,
        pl.BlockSpec((1,) + conv_shape[1:], lambda i: (group(i), 0, 0, 0)),
        pl.BlockSpec((1,) + ssm_shape[1:], lambda i: (group(i), 0, 0, 0, 0)),
        pl.BlockSpec((1,) + hg_shape[1:], lambda i: (group(i), 0, 0, 0, 0)),
    ]
    out_shape = [
        jax.ShapeDtypeStruct(h.shape, F32),
        jax.ShapeDtypeStruct(conv_shape, F32),
        jax.ShapeDtypeStruct(ssm_shape, F32),
        jax.ShapeDtypeStruct(hg_shape, F32),
    ]
    return pl.pallas_call(
        functools.partial(_mixer_kernel, layer=layer, n_prompt_blocks=n_prompt_blocks),
        grid=(n_steps,),
        in_specs=in_specs,
        out_specs=out_specs,
        out_shape=out_shape,
        scratch_shapes=[
            pltpu.VMEM((ROWS, IN_MAIN), F32),
            pltpu.VMEM((N_STREAMS, CONV_TILES, SUB + BLK, LANES), F32),
            pltpu.VMEM((ROWS, SSM_WIDTH), F32),
            pltpu.VMEM((ROWS, D_MODEL), BF16),
            pltpu.VMEM((SLAB_SLOTS, SUB * BLK, HG_WIDTH), F32),
        ],
        compiler_params=pltpu.CompilerParams(
            dimension_semantics=("arbitrary",), vmem_limit_bytes=VMEM_LIMIT_BYTES),
        name="mixer",
    )(h, p["ln_mix"], p["w_in"], p["w_x"], p["w_dt"], p["conv_w"], p["conv_b"], p["dt_bias"], p["a_log"],
      p["d_skip"], p["ssm_norm"], p["lb_raw"], p["hg_norm"], p["w_out"],
      _row_broadcast_matrix(), sconv, sssm, shg)


def _expand_heads(x):
    return jnp.repeat(x.astype(F32), SSM_HEADDIM)[None, :]


def kernel(x_prompt, x_sample, state_conv, state_ssm, state_hgrn, meta_tokens, ln_ffa_w, ffa_w_gate, ffa_w_up, ffa_w_down, ln_mix_w, w_in, conv_w, conv_b, dt_bias, a_log, d_skip, ssm_norm_w, hg_lb_raw, hg_norm_w, w_out, ln_ffb_w, ffb_w_gate, ffb_w_up, ffb_w_down, ln_f_w):
    depth = w_in.shape[0]
    nb, seq, _ = x_prompt.shape
    assert nb == N_STREAMS and x_sample.shape[:2] == (N_STREAMS, BLK) and seq % BLK == 0
    n_seq_blocks = seq // BLK
    n_prompt_blocks = n_seq_blocks + 1

    row = lambda a: a.astype(F32)[None, :]
    dt_cols = slice(SSM_WIDTH + CONV_CH, SSM_WIDTH + CONV_CH + SSM_WIDTH // SSM_HEADDIM)
    conv_out, ssm_out, hg_out = [], [], []
    for l in range(depth):
        w = w_in[l]
        params = {
            "ln_mix": row(ln_mix_w[l]),
            "w_in": jnp.concatenate([w[:, :SSM_WIDTH], w[:, dt_cols.stop:]], axis=1).astype(BF16),
            "w_x": w[:, SSM_WIDTH:dt_cols.start].astype(BF16),
            "w_dt": jnp.repeat(w[:, dt_cols], SSM_HEADDIM, axis=1).astype(BF16),
            "conv_w": conv_w[l].astype(F32),
            "conv_b": row(conv_b[l]),
            "dt_bias": _expand_heads(dt_bias[l]),
            "a_log": _expand_heads(a_log[l]),
            "d_skip": _expand_heads(d_skip[l]),
            "ssm_norm": row(ssm_norm_w[l]),
            "lb_raw": hg_lb_raw.astype(F32),
            "hg_norm": row(hg_norm_w[l]),
            "w_out": w_out[l].astype(BF16),
        }
        sconv = jnp.pad(state_conv[l].astype(F32), ((0, 0), (SUB - (CONV_WIDTH - 1), 0), (0, 0)))
        sssm = (state_ssm[l].astype(F32)
                .reshape(nb, SSM_GROUPS, SSM_GROUP_WIDTH // SSM_HEADDIM, SSM_HEADDIM, SSM_STATE)
                .transpose(0, 1, 4, 2, 3).reshape(nb, SSM_GROUPS, SSM_STATE, SSM_GROUP_WIDTH))
        shg = state_hgrn[l].astype(F32)

        ffa = (row(ln_ffa_w[l]), ffa_w_gate[l].astype(BF16), ffa_w_up[l].astype(BF16), ffa_w_down[l].astype(BF16))
        ffb = (row(ln_ffb_w[l]), ffb_w_gate[l].astype(BF16), ffb_w_up[l].astype(BF16), ffb_w_down[l].astype(BF16))
        if l == 0:
            h = _ffn_first(x_prompt.astype(F32), x_sample.astype(F32), meta_tokens.astype(F32), *ffa)
        else:
            h = _ffn(h, *ffa)
        h, oc, os_, og = _mixer(h, params, l, n_prompt_blocks, sconv, sssm, shg)
        if l == depth - 1:
            y_prompt, y_sample = _ffn_last(h, *ffb, row(ln_f_w))
        else:
            h = _ffn(h, *ffb)
        conv_out.append(oc[:, :, SUB - (CONV_WIDTH - 1):, :])
        ssm_out.append(os_.reshape(2, nb, SSM_GROUPS, SSM_STATE, SSM_GROUP_WIDTH // SSM_HEADDIM, SSM_HEADDIM)
                       .transpose(0, 1, 2, 4, 5, 3)
                       .reshape(2, nb, SSM_WIDTH // SSM_HEADDIM, SSM_HEADDIM, SSM_STATE))
        hg_out.append(og)

    conv_all, ssm_all, hg_all = jnp.stack(conv_out), jnp.stack(ssm_out), jnp.stack(hg_out)
    dt_ = x_prompt.dtype
    return (y_prompt.astype(dt_), y_sample.astype(x_sample.dtype),
            conv_all[:, 0].astype(dt_), ssm_all[:, 0].astype(dt_), hg_all[:, 0].astype(dt_),
            conv_all[:, 1].astype(state_conv.dtype), ssm_all[:, 1].astype(state_ssm.dtype),
            hg_all[:, 1].astype(state_hgrn.dtype))
```

```python
import functools

import numpy as np
import jax
import jax.numpy as jnp
from jax import lax
from jax.experimental import pallas as pl
from jax.experimental.pallas import tpu as pltpu

F32 = jnp.float32
BF16 = jnp.bfloat16

D_MODEL = 1024
N_STREAMS = 8
BLK = 64
ROWS = N_STREAMS * BLK
N_META = 16
N_PAD = BLK - N_META
SSM_WIDTH = 512
SSM_GROUPS = 2
SSM_GROUP_WIDTH = SSM_WIDTH // SSM_GROUPS
SSM_HEADDIM = 64
SSM_STATE = 128
CONV_CH = 1024
CONV_WIDTH = 4
HG_WIDTH = 512
HG_HEADS = 4
HG_DIM = 128
D_FF = 2816
MXU_TILE = 256
FF_CHUNK_EDGES = (0, 6 * MXU_TILE, D_FF)
EPS = 1e-6
LB_FLOOR = 1e-30
SUB = 8
LEVELS = (32, 16, 8)
LOG2E = 1.4426950408889634

N_PARTS = 2
STREAMS_PER_PART = N_STREAMS // N_PARTS
PART_ROWS = STREAMS_PER_PART * BLK
PROJ_CHUNK = 256
SLAB_SLOTS = 4
STAGGER = (1, 1)
LANES = 128
CONV_TILES = CONV_CH // LANES
BIG = float(2 ** 100)

C_Z, C_Q, C_F, C_I, C_G = 0, 512, 1024, 1536, 2048
IN_MAIN = 2560

VMEM_LIMIT_BYTES = 56 * 1024 * 1024


def _rms(x, w):
    return x * lax.rsqrt(jnp.mean(x * x, axis=-1, keepdims=True) + EPS) * w


def _silu(x):
    return x / (1.0 + jnp.exp(-x))


def _log1p(y):
    u = 1.0 + y
    return jnp.where(u == 1.0, y, jnp.log(u) * (y / (u - 1.0)))


def _softplus(x):
    return jnp.maximum(x, 0.0) + _log1p(jnp.exp(-jnp.abs(x)))


def _dot(a, b):
    return jnp.dot(a, b, preferred_element_type=F32)


def _dot_nt(a, b):
    return lax.dot_general(a, b, (((1,), (1,)), ((), ())), preferred_element_type=F32)


def _dot_tn(a, b):
    return lax.dot_general(a, b, (((0,), (0,)), ((), ())), preferred_element_type=F32)


def _embed_rows(piece, r0):
    s, w = piece.shape
    tile = 2 * SUB
    if s % tile:
        assert s == SUB and r0 % SUB == 0
        zeros = jnp.zeros((SUB, w), F32)
        piece = jnp.concatenate([zeros, piece] if r0 % tile else [piece, zeros], axis=0)
        r0, s = r0 - r0 % tile, tile
    assert r0 % tile == 0 and s % tile == 0
    parts = [piece.astype(BF16)]
    if r0:
        parts.insert(0, jnp.zeros((r0, w), BF16))
    if BLK - r0 - s:
        parts.append(jnp.zeros((BLK - r0 - s, w), BF16))
    return jnp.concatenate(parts, axis=0) if len(parts) > 1 else parts[0]


def _split3(x):
    hi = x.astype(BF16)
    r1 = x - hi.astype(F32)
    mid = r1.astype(BF16)
    lo = (r1 - mid.astype(F32)).astype(BF16)
    return jnp.concatenate([hi, mid, lo], axis=0)


def _swiglu_step(h, lnw_ref, wg_ref, wu_ref, wd_ref):
    hn = _rms(h, lnw_ref[...]).astype(BF16)
    acc = jnp.zeros_like(h)
    for c0, c1 in zip(FF_CHUNK_EDGES[:-1], FF_CHUNK_EDGES[1:]):
        cs = slice(c0, c1)
        g = _dot(hn, wg_ref[:, cs])
        u = _dot(hn, wu_ref[:, cs])
        a = (_silu(g) * u).astype(BF16)
        acc = acc + _dot(a, wd_ref[cs, :])
    return h + 0.5 * acc


def _ffn_kernel(h_ref, lnw_ref, wg_ref, wu_ref, wd_ref, o_ref):
    o_ref[...] = _swiglu_step(h_ref[...], lnw_ref, wg_ref, wu_ref, wd_ref)


def _ffn_first_kernel(xp_ref, xs_ref, meta_ref, lnw_ref, wg_ref, wu_ref, wd_ref, o_ref):
    step = pl.program_id(0)
    first = jnp.concatenate([jnp.zeros((N_PAD, D_MODEL), F32), meta_ref[...]], axis=0)
    h = jnp.where(step == 0, first[None], xp_ref[...])
    h = jnp.where(step == pl.num_programs(0) - 1, xs_ref[...], h)
    o_ref[...] = _swiglu_step(h.reshape(ROWS, D_MODEL), lnw_ref, wg_ref, wu_ref, wd_ref)


def _ffn_last_kernel(h_ref, lnw_ref, wg_ref, wu_ref, wd_ref, lnf_ref, yp_ref, ys_ref):
    step = pl.program_id(0)
    last = pl.num_programs(0) - 1
    y = _rms(_swiglu_step(h_ref[...], lnw_ref, wg_ref, wu_ref, wd_ref), lnf_ref[...])
    y = y.reshape(N_STREAMS, BLK, D_MODEL)

    @pl.when(step < last)
    def _():
        yp_ref[...] = y

    @pl.when(step == last)
    def _():
        ys_ref[...] = y


def _const_spec(shape):
    nd = len(shape)
    return pl.BlockSpec(shape, lambda i: (0,) * nd, pipeline_mode=pl.Buffered(1))


def _layer_spec(shape, layer):
    nd = len(shape)
    return pl.BlockSpec((None,) + shape, lambda i: (layer,) + (0,) * nd, pipeline_mode=pl.Buffered(1))


def _ffn_weight_specs(layer):
    return [_const_spec((1, D_MODEL)), _layer_spec((D_MODEL, D_FF), layer),
            _layer_spec((D_MODEL, D_FF), layer), _layer_spec((D_FF, D_MODEL), layer)]


_ROW_SPEC = pl.BlockSpec((ROWS, D_MODEL), lambda i: (i, 0))
_FFN_PARAMS = pltpu.CompilerParams(dimension_semantics=("arbitrary",), vmem_limit_bytes=VMEM_LIMIT_BYTES)


def _prompt_block_spec(n_seq_blocks):
    return pl.BlockSpec((N_STREAMS, BLK, D_MODEL),
                        lambda i: (0, jnp.clip(i - 1, 0, n_seq_blocks - 1), 0))


def _ffn(h, layer, lnw, wg, wu, wd):
    return pl.pallas_call(
        _ffn_kernel,
        grid=(h.shape[0] // ROWS,),
        in_specs=[_ROW_SPEC] + _ffn_weight_specs(layer),
        out_specs=_ROW_SPEC,
        out_shape=jax.ShapeDtypeStruct(h.shape, F32),
        compiler_params=_FFN_PARAMS,
        name="ffn",
    )(h, lnw, wg, wu, wd)


def _ffn_first(x_prompt, x_sample, meta, layer, lnw, wg, wu, wd):
    n_seq_blocks = x_prompt.shape[1] // BLK
    n_steps = n_seq_blocks + 2
    return pl.pallas_call(
        _ffn_first_kernel,
        grid=(n_steps,),
        in_specs=[_prompt_block_spec(n_seq_blocks), _const_spec((N_STREAMS, BLK, D_MODEL)),
                  _const_spec((N_META, D_MODEL))] + _ffn_weight_specs(layer),
        out_specs=_ROW_SPEC,
        out_shape=jax.ShapeDtypeStruct((n_steps * ROWS, D_MODEL), F32),
        compiler_params=_FFN_PARAMS,
        name="ffn_first",
    )(x_prompt, x_sample, meta, lnw, wg, wu, wd)


def _ffn_last(h, layer, lnw, wg, wu, wd, lnf):
    n_steps = h.shape[0] // ROWS
    n_seq_blocks = n_steps - 2
    return pl.pallas_call(
        _ffn_last_kernel,
        grid=(n_steps,),
        in_specs=[_ROW_SPEC] + _ffn_weight_specs(layer) + [_const_spec((1, D_MODEL))],
        out_specs=[_prompt_block_spec(n_seq_blocks),
                   pl.BlockSpec((N_STREAMS, BLK, D_MODEL), lambda i: (0, 0, 0))],
        out_shape=[jax.ShapeDtypeStruct((N_STREAMS, n_seq_blocks * BLK, D_MODEL), F32),
                   jax.ShapeDtypeStruct((N_STREAMS, BLK, D_MODEL), F32)],
        compiler_params=_FFN_PARAMS,
        name="ffn_last",
    )(h, lnw, wg, wu, wd, lnf)


def _mixer_kernel(h_ref, lnw_ref, win_ref, wx_ref, wdt_ref, convw_ref, convb_ref, dtb_ref, alog_ref,
                  dskip_ref, ssmnw_ref, lbraw_ref, hgnw_ref, wout_ref, bmat_ref,
                  sconv_ref, sssm_ref, shg_ref,
                  hout_ref, oconv_ref, ossm_ref, ohg_ref,
                  proj_ref, xbc_ref, dtraw_ref, mix_ref, gb_ref, *, layer, n_prompt_blocks):
    step = pl.program_id(0)

    @pl.when(step == 0)
    def _():
        oconv_ref[...] = jnp.zeros(oconv_ref.shape, F32)
        ossm_ref[...] = jnp.zeros(ossm_ref.shape, F32)
        ohg_ref[...] = jnp.zeros(ohg_ref.shape, F32)

    @pl.when(step == n_prompt_blocks)
    def _():
        oconv_ref[0] = sconv_ref[...]
        ossm_ref[0] = sssm_ref[...]
        ohg_ref[0] = shg_ref[...]

    ready = [set() for _ in range(N_PARTS)]

    def in_projection(part):
        r0 = part * PART_ROWS
        rs = slice(r0, r0 + PART_ROWS)
        hn = _rms(h_ref[rs, :], lnw_ref[...]).astype(BF16)
        yield
        for c0 in range(0, CONV_CH, PROJ_CHUNK):
            x_new = _dot(hn, wx_ref[:, c0:c0 + PROJ_CHUNK])
            for k in range(STREAMS_PER_PART):
                for lt in range(PROJ_CHUNK // LANES):
                    xbc_ref[part * STREAMS_PER_PART + k, c0 // LANES + lt, SUB:SUB + BLK, :] = (
                        x_new[k * BLK:(k + 1) * BLK, lt * LANES:(lt + 1) * LANES])
            yield
        ready[part].add("x")
        for c0 in range(0, SSM_WIDTH, PROJ_CHUNK):
            dtraw_ref[rs, c0:c0 + PROJ_CHUNK] = _dot(hn, wdt_ref[:, c0:c0 + PROJ_CHUNK])
            yield
        ready[part].add("dt")
        for name, c_lo in (("z", C_Z), ("q", C_Q), ("f", C_F), ("i", C_I), ("g", C_G)):
            for c0 in range(c_lo, c_lo + HG_WIDTH, PROJ_CHUNK):
                proj_ref[rs, c0:c0 + PROJ_CHUNK] = _dot(hn, win_ref[:, c0:c0 + PROJ_CHUNK])
                yield
            ready[part].add(name)

    def out_projection(part):
        r0 = part * PART_ROWS
        rs = slice(r0, r0 + PART_ROWS)
        for c0 in range(0, D_MODEL, PROJ_CHUNK):
            cs = slice(c0, c0 + PROJ_CHUNK)
            hout_ref[rs, cs] = h_ref[rs, cs] + _dot(mix_ref[rs, :], wout_ref[:, cs])
            yield

    first_valid = jnp.where(step == 0, N_PAD, 0)
    t_col = lax.broadcasted_iota(jnp.int32, (BLK, 1), 0)
    valid = t_col >= first_valid

    tri3 = (lax.broadcasted_iota(jnp.int32, (BLK, 3 * BLK), 0)
            >= lax.broadcasted_iota(jnp.int32, (BLK, 3 * BLK), 1) % BLK).astype(BF16)

    conv_w = convw_ref[...]
    conv_b = convb_ref[...]
    a2 = -jnp.exp(alog_ref[...]) * LOG2E
    row_w = lax.broadcasted_iota(jnp.int32, (BLK, SSM_WIDTH), 0)
    col_w = lax.broadcasted_iota(jnp.int32, (BLK, SSM_WIDTH), 1)
    diag_pick = (col_w % BLK) == row_w
    row_g = lax.broadcasted_iota(jnp.int32, (BLK, SSM_GROUP_WIDTH), 0)
    col_g = lax.broadcasted_iota(jnp.int32, (BLK, SSM_GROUP_WIDTH), 1)
    causal_g = (col_g % BLK) <= row_g
    bd_r = lax.broadcasted_iota(jnp.int32, (SSM_GROUP_WIDTH, SSM_GROUP_WIDTH), 0)
    bd_c = lax.broadcasted_iota(jnp.int32, (SSM_GROUP_WIDTH, SSM_GROUP_WIDTH), 1)
    head_diag = ((bd_r // BLK) == (bd_c // BLK)).astype(BF16)

    raw = lbraw_ref[...]
    e_raw = jnp.exp(raw - jnp.max(raw, axis=0, keepdims=True))
    sm = e_raw / jnp.sum(e_raw, axis=0, keepdims=True)
    lb = jnp.clip(jnp.sum(sm[0:layer + 1], axis=0, keepdims=True) - sm[0:1], 0.0, 1.0 - 1e-6)
    log_lb = jnp.log(jnp.maximum(lb, LB_FLOOR))
    log_1m_lb = jnp.log1p(-lb)
    big_rows = jnp.where(lax.broadcasted_iota(jnp.int32, (BLK, HG_WIDTH), 0) == 0, BIG, 0.0).astype(BF16)
    piece_col = lax.broadcasted_iota(jnp.int32, (SUB, BLK), 1)

    def per_stream(b, slot):
        r0 = b * BLK if isinstance(b, int) else pl.multiple_of(b * BLK, BLK)
        rows = pl.ds(r0, BLK)
        projected = ready[b // STREAMS_PER_PART]

        def wait_for(name):
            while name not in projected:
                yield

        yield from wait_for("x")
        tail = oconv_ref[0, b]
        conv_parts = []
        for lt in range(CONV_TILES):
            ls = slice(lt * LANES, (lt + 1) * LANES)
            xbc_ref[b, lt, 0:SUB, :] = tail[:, ls]
            part = conv_b[:, ls]
            for s in range(CONV_WIDTH - 1, -1, -1):
                tap = xbc_ref[b, lt, pl.ds(SUB - s, BLK, stride=1), :]
                part = part + tap * conv_w[CONV_WIDTH - 1 - s:CONV_WIDTH - s, ls]
            conv_parts.append(part)
        oconv_ref[0, b] = jnp.concatenate(
            [xbc_ref[b, lt, BLK:BLK + SUB, :] for lt in range(CONV_TILES)], axis=1)
        xbc = _silu(jnp.concatenate(conv_parts, axis=1))
        xs = xbc[:, 0:SSM_WIDTH]
        yield

        yield from wait_for("dt")
        dt = _softplus(dtraw_ref[rows, :] + dtb_ref[...])
        dt = jnp.where(valid, dt, 0.0)
        decay_pieces = _split3(dt * a2)
        groups = [slice(g * SSM_GROUP_WIDTH, (g + 1) * SSM_GROUP_WIDTH) for g in range(SSM_GROUPS)]
        b16 = [xbc[:, SSM_WIDTH + g * SSM_STATE:SSM_WIDTH + (g + 1) * SSM_STATE].astype(BF16)
               for g in range(SSM_GROUPS)]
        c16 = [xbc[:, SSM_WIDTH + (SSM_GROUPS + g) * SSM_STATE:
                   SSM_WIDTH + (SSM_GROUPS + g + 1) * SSM_STATE].astype(BF16) for g in range(SSM_GROUPS)]
        xdt = [xs[:, cs] * dt[:, cs] for cs in groups]
        xdt_bd = [jnp.concatenate([x.astype(BF16)] * 4, axis=0) * head_diag for x in xdt]
        st16 = [ossm_ref[0, b, g].astype(BF16) for g in range(SSM_GROUPS)]
        yield
        cum = _dot(tri3, decay_pieces)
        cbm = [_dot_nt(c16[g], jnp.concatenate([b16[g]] * 4, axis=0)) for g in range(SSM_GROUPS)]
        y_state = [_dot(c16[g], st16[g]) for g in range(SSM_GROUPS)]
        yield
        cum_row = jnp.sum(jnp.where(diag_pick, cum, 0.0), axis=0, keepdims=True)
        att16, xdt_w, last = [], [], []
        for g, cs in enumerate(groups):
            decay = jnp.where(causal_g, jnp.exp2(cum[:, cs] - cum_row[:, cs]), 0.0)
            att16.append((cbm[g] * decay).astype(BF16))
            last.append(cum[BLK - 1:BLK, cs])
            xdt_w.append((xdt[g] * jnp.exp2(last[g] - cum[:, cs])).astype(BF16))
        yield
        y_intra = [_dot(att16[g], xdt_bd[g]) for g in range(SSM_GROUPS)]
        st_new = [_dot_tn(b16[g], xdt_w[g]) for g in range(SSM_GROUPS)]
        yield
        yield from wait_for("z")
        y_parts = []
        for g, cs in enumerate(groups):
            ossm_ref[0, b, g] = jnp.exp2(last[g]) * ossm_ref[0, b, g] + st_new[g]
            y_g = y_intra[g] + y_state[g] * jnp.exp2(cum[:, cs]) + dskip_ref[:, cs] * xs[:, cs]
            y_g = y_g * _silu(proj_ref[rows, C_Z + g * SSM_GROUP_WIDTH:C_Z + (g + 1) * SSM_GROUP_WIDTH])
            y_g = y_g * lax.rsqrt(jnp.mean(y_g * y_g, axis=-1, keepdims=True) + EPS)
            y_parts.append(y_g * ssmnw_ref[:, cs])
        mix_ref[rows, 0:SSM_WIDTH] = jnp.concatenate(y_parts, axis=1).astype(BF16)
        yield

        yield from wait_for("f")
        q = _silu(proj_ref[rows, C_Q:C_Q + HG_WIDTH])
        fr = proj_ref[rows, C_F:C_F + HG_WIDTH]
        log_sig = jnp.minimum(fr, 0.0) - jnp.log(1.0 + jnp.exp(-jnp.abs(fr)))
        t2 = log_1m_lb + log_sig
        lf = jnp.maximum(log_lb, t2) + jnp.log(1.0 + jnp.exp(-jnp.abs(log_lb - t2)))
        lf = jnp.where(valid, lf, 0.0)
        log2_key = jnp.where(valid, (t2 - fr) * LOG2E, -BIG)
        forget_pieces = _split3(lf * LOG2E)
        yield
        cumh = _dot(tri3, forget_pieces)
        yield
        g = cumh - log2_key
        lasth = cumh[BLK - 1:BLK]
        q_in = (q * jnp.exp2(cumh)).astype(BF16)
        k_out = jnp.exp2(lasth - g).astype(BF16)
        g_pieces = jnp.concatenate([_split3(g), big_rows], axis=0)

        q_cols, k_cols = [], []
        for s in LEVELS:
            for m in range(0, BLK, 2 * s):
                lo, up = slice(m, m + s), slice(m + s, m + 2 * s)
                mid = cumh[m + s - 1:m + s]
                k_cols.append(_embed_rows(jnp.exp2(mid - g[lo]), m))
                q_cols.append(_embed_rows(q[up] * jnp.exp2(cumh[up] - mid), m + s))
        yield
        gb_ref[slot] = _dot(bmat_ref[...], g_pieces)
        att_off = []
        for hh in range(HG_HEADS):
            hs = slice(hh * HG_DIM, (hh + 1) * HG_DIM)
            att_off.append(_dot_nt(jnp.concatenate([c[:, hs] for c in q_cols], axis=1),
                                   jnp.concatenate([c[:, hs] for c in k_cols], axis=1)))
        yield
        diag_pieces = []
        for u in range(BLK // SUB):
            rs = slice(u * SUB, (u + 1) * SUB)
            pieces = [jnp.zeros((SUB, BLK), F32) for _ in range(HG_HEADS)]
            for j in range(SUB):
                g_j = gb_ref[slot, u * BLK + j * SUB:u * BLK + (j + 1) * SUB, :]
                slab = q[rs] * jnp.exp2(cumh[rs] - g_j)
                for hh in range(HG_HEADS):
                    pair_sum = jnp.sum(slab[:, hh * HG_DIM:(hh + 1) * HG_DIM], axis=-1, keepdims=True)
                    pieces[hh] = jnp.where(piece_col == u * SUB + j, pair_sum, pieces[hh])
            diag_pieces.append(pieces)
            if u % 2:
                yield
        yield from wait_for("i")
        v16 = proj_ref[rows, C_I:C_I + HG_WIDTH].astype(BF16)
        decay_rows = jnp.exp2(lasth)
        heads = [slice(hh * HG_DIM, (hh + 1) * HG_DIM) for hh in range(HG_HEADS)]
        lhs, rhs, decay_cols = [], [], []
        for hh, hs in enumerate(heads):
            att_d = jnp.concatenate([diag_pieces[u][hh] for u in range(BLK // SUB)], axis=0)
            a_h = (att_off[hh] + att_d).astype(BF16)
            lhs.append(jnp.concatenate([q_in[:, hs], a_h], axis=1))
            rhs.append(jnp.concatenate([ohg_ref[0, b, hh].astype(BF16), v16[:, hs]], axis=0))
            decay_cols.append(jnp.broadcast_to(decay_rows[:, hs], (HG_DIM, HG_DIM)).T)
        yield
        o_parts = [_dot(lhs[hh], rhs[hh]) for hh in range(HG_HEADS)]
        kv_new = [_dot_tn(k_out[:, hs], v16[:, hs]) for hs in heads]
        yield
        for hh in range(HG_HEADS):
            ohg_ref[0, b, hh] = ohg_ref[0, b, hh] * decay_cols[hh] + kv_new[hh]
        yield from wait_for("g")
        o_parts = [o_h * lax.rsqrt(jnp.mean(o_h * o_h, axis=-1, keepdims=True) + EPS) for o_h in o_parts]
        o = jnp.concatenate(o_parts, axis=1) * hgnw_ref[...]
        o = o * _silu(proj_ref[rows, C_G:C_G + HG_WIDTH])
        mix_ref[rows, SSM_WIDTH:SSM_WIDTH + HG_WIDTH] = o.astype(BF16)

    tasks = {}
    for part in range(N_PARTS):
        first = part * STREAMS_PER_PART
        streams = ["s%d" % b for b in range(first, first + STREAMS_PER_PART)]
        tasks["in%d" % part] = (in_projection(part), [] if part == 0 else ["in%d" % (part - 1)])
        for b in range(first, first + STREAMS_PER_PART):
            tasks["s%d" % b] = (per_stream(b, b % SLAB_SLOTS), [])
        tasks["out%d" % part] = (out_projection(part), streams)
    done, active, rnd, next_stream_start = set(), [], 0, 0
    pending = list(tasks)
    while pending or active:
        for name in list(pending):
            is_stream = name.startswith("s")
            if all(dep in done for dep in tasks[name][1]) and not (is_stream and rnd < next_stream_start):
                pending.remove(name)
                active.append(name)
                if is_stream:
                    next_stream_start = rnd + STAGGER[int(name[1:]) // STREAMS_PER_PART]
        for name in list(active):
            if next(tasks[name][0], True) is not None:
                active.remove(name)
                done.add(name)
        rnd += 1


def _row_broadcast_matrix():
    r = np.arange(SUB * BLK)
    u, j, i = r // BLK, (r % BLK) // SUB, r % SUB
    c = np.arange(4 * BLK)
    part, t = c // BLK, c % BLK
    pick = (part[None, :] < 3) & (t[None, :] == (u * SUB + j)[:, None])
    mask = (part[None, :] == 3) & (t[None, :] == 0) & (i < j)[:, None]
    return jnp.asarray(pick | mask, dtype=BF16)


def _mixer(h, p, layer, n_prompt_blocks, sconv, sssm, shg):
    n_steps = h.shape[0] // ROWS
    row_spec = pl.BlockSpec((ROWS, D_MODEL), lambda i: (i, 0))
    group = lambda i: i // n_prompt_blocks
    conv_shape = (2, N_STREAMS, SUB, CONV_CH)
    ssm_shape = (2, N_STREAMS, SSM_GROUPS, SSM_STATE, SSM_GROUP_WIDTH)
    hg_shape = (2, N_STREAMS, HG_HEADS, HG_DIM, HG_DIM)
    in_specs = [
        row_spec,
        _const_spec((1, D_MODEL)),
        _layer_spec((D_MODEL, IN_MAIN), layer),
        _layer_spec((D_MODEL, CONV_CH), layer),
        _layer_spec((D_MODEL, SSM_WIDTH), layer),
        _const_spec((CONV_WIDTH, CONV_CH)),
        _const_spec((1, CONV_CH)),
        _const_spec((1, SSM_WIDTH)),
        _const_spec((1, SSM_WIDTH)),
        _const_spec((1, SSM_WIDTH)),
        _const_spec((1, SSM_WIDTH)),
        _const_spec((p["lb_raw"].shape[0], HG_WIDTH)),
        _const_spec((1, HG_WIDTH)),
        _layer_spec((D_MODEL, D_MODEL), layer),
        _const_spec((SUB * BLK, 4 * BLK)),
        _layer_spec(conv_shape[1:], layer),
        _layer_spec(ssm_shape[1:], layer),
        _layer_spec(hg_shape[1:], layer),
    ]
    out_specs = [
        row_spec,
        pl.BlockSpec((1,) + conv_shape[1:], lambda i: (group(i), 0, 0, 0)),
        pl.BlockSpec((1,) + ssm_shape[1:], lambda i: (group(i), 0, 0, 0, 0)),
        pl.BlockSpec((1,) + hg_shape[1:], lambda i: (group(i), 0, 0, 0, 0)),
    ]
    out_shape = [
        jax.ShapeDtypeStruct(h.shape, F32),
        jax.ShapeDtypeStruct(conv_shape, F32),
        jax.ShapeDtypeStruct(ssm_shape, F32),
        jax.ShapeDtypeStruct(hg_shape, F32),
    ]
    return pl.pallas_call(
        functools.partial(_mixer_kernel, layer=layer, n_prompt_blocks=n_prompt_blocks),
        grid=(n_steps,),
        in_specs=in_specs,
        out_specs=out_specs,
        out_shape=out_shape,
        scratch_shapes=[
            pltpu.VMEM((ROWS, IN_MAIN), F32),
            pltpu.VMEM((N_STREAMS, CONV_TILES, SUB + BLK, LANES), F32),
            pltpu.VMEM((ROWS, SSM_WIDTH), F32),
            pltpu.VMEM((ROWS, D_MODEL), BF16),
            pltpu.VMEM((SLAB_SLOTS, SUB * BLK, HG_WIDTH), F32),
        ],
        compiler_params=pltpu.CompilerParams(
            dimension_semantics=("arbitrary",), vmem_limit_bytes=VMEM_LIMIT_BYTES),
        name="mixer",
    )(h, p["ln_mix"], p["w_in"], p["w_x"], p["w_dt"], p["conv_w"], p["conv_b"], p["dt_bias"], p["a_log"],
      p["d_skip"], p["ssm_norm"], p["lb_raw"], p["hg_norm"], p["w_out"],
      _row_broadcast_matrix(), sconv, sssm, shg)


def _expand_heads(x):
    return jnp.repeat(x.astype(F32), SSM_HEADDIM)[None, :]


def kernel(x_prompt, x_sample, state_conv, state_ssm, state_hgrn, meta_tokens, ln_ffa_w, ffa_w_gate, ffa_w_up, ffa_w_down, ln_mix_w, w_in, conv_w, conv_b, dt_bias, a_log, d_skip, ssm_norm_w, hg_lb_raw, hg_norm_w, w_out, ln_ffb_w, ffb_w_gate, ffb_w_up, ffb_w_down, ln_f_w):
    depth = w_in.shape[0]
    nb, seq, _ = x_prompt.shape
    assert nb == N_STREAMS and x_sample.shape[:2] == (N_STREAMS, BLK) and seq % BLK == 0
    n_seq_blocks = seq // BLK
    n_prompt_blocks = n_seq_blocks + 1

    row = lambda a: a.astype(F32)[None, :]
    dt_cols = slice(SSM_WIDTH + CONV_CH, SSM_WIDTH + CONV_CH + SSM_WIDTH // SSM_HEADDIM)
    w_main16 = jnp.concatenate([w_in[:, :, :SSM_WIDTH], w_in[:, :, dt_cols.stop:]], axis=2).astype(BF16)
    w_x16 = w_in[:, :, SSM_WIDTH:dt_cols.start].astype(BF16)
    w_dt16 = jnp.repeat(w_in[:, :, dt_cols], SSM_HEADDIM, axis=2).astype(BF16)
    w_out16 = w_out.astype(BF16)
    ffa16 = (ffa_w_gate.astype(BF16), ffa_w_up.astype(BF16), ffa_w_down.astype(BF16))
    ffb16 = (ffb_w_gate.astype(BF16), ffb_w_up.astype(BF16), ffb_w_down.astype(BF16))
    sconv_all = jnp.pad(state_conv.astype(F32), ((0, 0), (0, 0), (SUB - (CONV_WIDTH - 1), 0), (0, 0)))
    sssm_all = (state_ssm.astype(F32)
                .reshape(depth, nb, SSM_GROUPS, SSM_GROUP_WIDTH // SSM_HEADDIM, SSM_HEADDIM, SSM_STATE)
                .transpose(0, 1, 2, 5, 3, 4).reshape(depth, nb, SSM_GROUPS, SSM_STATE, SSM_GROUP_WIDTH))
    shg_all = state_hgrn.astype(F32)

    conv_out, ssm_out, hg_out = [], [], []
    for l in range(depth):
        params = {
            "ln_mix": row(ln_mix_w[l]),
            "w_in": w_main16,
            "w_x": w_x16,
            "w_dt": w_dt16,
            "conv_w": conv_w[l].astype(F32),
            "conv_b": row(conv_b[l]),
            "dt_bias": _expand_heads(dt_bias[l]),
            "a_log": _expand_heads(a_log[l]),
            "d_skip": _expand_heads(d_skip[l]),
            "ssm_norm": row(ssm_norm_w[l]),
            "lb_raw": hg_lb_raw.astype(F32),
            "hg_norm": row(hg_norm_w[l]),
            "w_out": w_out16,
        }
        if l == 0:
            h = _ffn_first(x_prompt.astype(F32), x_sample.astype(F32), meta_tokens.astype(F32),
                           l, row(ln_ffa_w[l]), *ffa16)
        else:
            h = _ffn(h, l, row(ln_ffa_w[l]), *ffa16)
        h, oc, os_, og = _mixer(h, params, l, n_prompt_blocks, sconv_all, sssm_all, shg_all)
        if l == depth - 1:
            y_prompt, y_sample = _ffn_last(h, l, row(ln_ffb_w[l]), *ffb16, row(ln_f_w))
        else:
            h = _ffn(h, l, row(ln_ffb_w[l]), *ffb16)
        conv_out.append(oc)
        ssm_out.append(os_)
        hg_out.append(og)

    conv_all = jnp.stack(conv_out)[:, :, :, SUB - (CONV_WIDTH - 1):, :]
    ssm_all = (jnp.stack(ssm_out)
               .reshape(depth, 2, nb, SSM_GROUPS, SSM_STATE, SSM_GROUP_WIDTH // SSM_HEADDIM, SSM_HEADDIM)
               .transpose(0, 1, 2, 3, 5, 6, 4)
               .reshape(depth, 2, nb, SSM_WIDTH // SSM_HEADDIM, SSM_HEADDIM, SSM_STATE))
    hg_all = jnp.stack(hg_out)
    dt_ = x_prompt.dtype
    return (y_prompt.astype(dt_), y_sample.astype(x_sample.dtype),
            conv_all[:, 0].astype(dt_), ssm_all[:, 0].astype(dt_), hg_all[:, 0].astype(dt_),
            conv_all[:, 1].astype(state_conv.dtype), ssm_all[:, 1].astype(state_ssm.dtype),
            hg_all[:, 1].astype(state_hgrn.dtype))
```

```python
import functools

import numpy as np
import jax
import jax.numpy as jnp
from jax import lax
from jax.experimental import pallas as pl
from jax.experimental.pallas import tpu as pltpu

F32 = jnp.float32
BF16 = jnp.bfloat16

D_MODEL = 1024
N_STREAMS = 8
BLK = 64
ROWS = N_STREAMS * BLK
N_META = 16
N_PAD = BLK - N_META
SSM_WIDTH = 512
SSM_GROUPS = 2
SSM_GROUP_WIDTH = SSM_WIDTH // SSM_GROUPS
SSM_HEADDIM = 64
SSM_STATE = 128
CONV_CH = 1024
CONV_WIDTH = 4
HG_WIDTH = 512
HG_HEADS = 4
HG_DIM = 128
D_FF = 2816
MXU_TILE = 256
FF_CHUNK_EDGES = (0, 6 * MXU_TILE, D_FF)
EPS = 1e-6
LB_FLOOR = 1e-30
SUB = 8
LEVELS = (32, 16, 8)
LOG2E = 1.4426950408889634

N_PARTS = 2
STREAMS_PER_PART = N_STREAMS // N_PARTS
PART_ROWS = STREAMS_PER_PART * BLK
PROJ_CHUNK = 256
SLAB_SLOTS = 4
STAGGER = (1, 1)
LANES = 128
CONV_TILES = CONV_CH // LANES
BIG = float(2 ** 100)

C_Z, C_Q, C_F, C_I, C_G = 0, 512, 1024, 1536, 2048
IN_MAIN = 2560

VMEM_LIMIT_BYTES = 56 * 1024 * 1024


def _rms(x, w):
    return x * lax.rsqrt(jnp.mean(x * x, axis=-1, keepdims=True) + EPS) * w


def _silu(x):
    return x / (1.0 + jnp.exp(-x))


def _log1p(y):
    u = 1.0 + y
    return jnp.where(u == 1.0, y, jnp.log(u) * (y / (u - 1.0)))


def _softplus(x):
    return jnp.maximum(x, 0.0) + _log1p(jnp.exp(-jnp.abs(x)))


def _dot(a, b):
    return jnp.dot(a, b, preferred_element_type=F32)


def _dot_nt(a, b):
    return lax.dot_general(a, b, (((1,), (1,)), ((), ())), preferred_element_type=F32)


def _dot_tn(a, b):
    return lax.dot_general(a, b, (((0,), (0,)), ((), ())), preferred_element_type=F32)


def _embed_rows(piece, r0):
    s, w = piece.shape
    tile = 2 * SUB
    if s % tile:
        assert s == SUB and r0 % SUB == 0
        zeros = jnp.zeros((SUB, w), F32)
        piece = jnp.concatenate([zeros, piece] if r0 % tile else [piece, zeros], axis=0)
        r0, s = r0 - r0 % tile, tile
    assert r0 % tile == 0 and s % tile == 0
    parts = [piece.astype(BF16)]
    if r0:
        parts.insert(0, jnp.zeros((r0, w), BF16))
    if BLK - r0 - s:
        parts.append(jnp.zeros((BLK - r0 - s, w), BF16))
    return jnp.concatenate(parts, axis=0) if len(parts) > 1 else parts[0]


def _split3(x):
    hi = x.astype(BF16)
    r1 = x - hi.astype(F32)
    mid = r1.astype(BF16)
    lo = (r1 - mid.astype(F32)).astype(BF16)
    return jnp.concatenate([hi, mid, lo], axis=0)


def _swiglu_step(h, lnw_ref, wg_ref, wu_ref, wd_ref):
    hn = _rms(h, lnw_ref[...]).astype(BF16)
    acc = jnp.zeros_like(h)
    for c0, c1 in zip(FF_CHUNK_EDGES[:-1], FF_CHUNK_EDGES[1:]):
        cs = slice(c0, c1)
        g = _dot(hn, wg_ref[:, cs])
        u = _dot(hn, wu_ref[:, cs])
        a = (_silu(g) * u).astype(BF16)
        acc = acc + _dot(a, wd_ref[cs, :])
    return h + 0.5 * acc


def _fetch_weight_bf16(w_hbm, layer, w16_ref, stage_ref, sem):
    rows = stage_ref.shape[1]
    n_chunks = w16_ref.shape[0] // rows

    def chunk_copy(k):
        return pltpu.make_async_copy(w_hbm.at[layer, pl.ds(k * rows, rows), :],
                                     stage_ref.at[k % 2], sem.at[k % 2])

    chunk_copy(0).start()
    for k in range(n_chunks):
        if k + 1 < n_chunks:
            chunk_copy(k + 1).start()
        chunk_copy(k).wait()
        w16_ref[k * rows:(k + 1) * rows, :] = stage_ref[k % 2].astype(BF16)


def _fetch_ffn_weights(layer, wg_hbm, wu_hbm, wd_hbm, wg_ref, wu_ref, wd_ref, wide_stage, wide_sem,
                       narrow_stage, narrow_sem):
    @pl.when(pl.program_id(0) == 0)
    def _():
        _fetch_weight_bf16(wg_hbm, layer, wg_ref, wide_stage, wide_sem)
        _fetch_weight_bf16(wu_hbm, layer, wu_ref, wide_stage, wide_sem)
        _fetch_weight_bf16(wd_hbm, layer, wd_ref, narrow_stage, narrow_sem)


def _ffn_kernel(h_ref, lnw_ref, wg_hbm, wu_hbm, wd_hbm, o_ref, wg_ref, wu_ref, wd_ref, *stage, layer):
    _fetch_ffn_weights(layer, wg_hbm, wu_hbm, wd_hbm, wg_ref, wu_ref, wd_ref, *stage)
    o_ref[...] = _swiglu_step(h_ref[...], lnw_ref, wg_ref, wu_ref, wd_ref)


def _ffn_first_kernel(xp_ref, xs_ref, meta_ref, lnw_ref, wg_hbm, wu_hbm, wd_hbm, o_ref,
                      wg_ref, wu_ref, wd_ref, *stage, layer):
    _fetch_ffn_weights(layer, wg_hbm, wu_hbm, wd_hbm, wg_ref, wu_ref, wd_ref, *stage)
    step = pl.program_id(0)
    first = jnp.concatenate([jnp.zeros((N_PAD, D_MODEL), F32), meta_ref[...]], axis=0)
    h = jnp.where(step == 0, first[None], xp_ref[...])
    h = jnp.where(step == pl.num_programs(0) - 1, xs_ref[...], h)
    o_ref[...] = _swiglu_step(h.reshape(ROWS, D_MODEL), lnw_ref, wg_ref, wu_ref, wd_ref)


def _ffn_last_kernel(h_ref, lnw_ref, wg_hbm, wu_hbm, wd_hbm, lnf_ref, yp_ref, ys_ref,
                     wg_ref, wu_ref, wd_ref, *stage, layer):
    _fetch_ffn_weights(layer, wg_hbm, wu_hbm, wd_hbm, wg_ref, wu_ref, wd_ref, *stage)
    step = pl.program_id(0)
    last = pl.num_programs(0) - 1
    y = _rms(_swiglu_step(h_ref[...], lnw_ref, wg_ref, wu_ref, wd_ref), lnf_ref[...])
    y = y.reshape(N_STREAMS, BLK, D_MODEL)

    @pl.when(step < last)
    def _():
        yp_ref[...] = y

    @pl.when(step == last)
    def _():
        ys_ref[...] = y


def _const_spec(shape):
    nd = len(shape)
    return pl.BlockSpec(shape, lambda i: (0,) * nd, pipeline_mode=pl.Buffered(1))


def _layer_spec(shape, layer):
    nd = len(shape)
    return pl.BlockSpec((None,) + shape, lambda i: (layer,) + (0,) * nd, pipeline_mode=pl.Buffered(1))


_HBM_SPEC = pl.BlockSpec(memory_space=pl.ANY)
_FFN_WEIGHT_SPECS = [_const_spec((1, D_MODEL)), _HBM_SPEC, _HBM_SPEC, _HBM_SPEC]
FETCH_ROWS = MXU_TILE
_FFN_SCRATCH = [
    pltpu.VMEM((D_MODEL, D_FF), BF16), pltpu.VMEM((D_MODEL, D_FF), BF16), pltpu.VMEM((D_FF, D_MODEL), BF16),
    pltpu.VMEM((2, FETCH_ROWS, D_FF), F32), pltpu.SemaphoreType.DMA((2,)),
    pltpu.VMEM((2, FETCH_ROWS, D_MODEL), F32), pltpu.SemaphoreType.DMA((2,)),
]
_ROW_SPEC = pl.BlockSpec((ROWS, D_MODEL), lambda i: (i, 0))
_FFN_PARAMS = pltpu.CompilerParams(dimension_semantics=("arbitrary",), vmem_limit_bytes=VMEM_LIMIT_BYTES)


def _prompt_block_spec(n_seq_blocks):
    return pl.BlockSpec((N_STREAMS, BLK, D_MODEL),
                        lambda i: (0, jnp.clip(i - 1, 0, n_seq_blocks - 1), 0))


def _ffn(h, layer, lnw, wg, wu, wd):
    return pl.pallas_call(
        functools.partial(_ffn_kernel, layer=layer),
        grid=(h.shape[0] // ROWS,),
        in_specs=[_ROW_SPEC] + _FFN_WEIGHT_SPECS,
        out_specs=_ROW_SPEC,
        out_shape=jax.ShapeDtypeStruct(h.shape, F32),
        scratch_shapes=_FFN_SCRATCH,
        compiler_params=_FFN_PARAMS,
        name="ffn",
    )(h, lnw, wg, wu, wd)


def _ffn_first(x_prompt, x_sample, meta, layer, lnw, wg, wu, wd):
    n_seq_blocks = x_prompt.shape[1] // BLK
    n_steps = n_seq_blocks + 2
    return pl.pallas_call(
        functools.partial(_ffn_first_kernel, layer=layer),
        grid=(n_steps,),
        in_specs=[_prompt_block_spec(n_seq_blocks), _const_spec((N_STREAMS, BLK, D_MODEL)),
                  _const_spec((N_META, D_MODEL))] + _FFN_WEIGHT_SPECS,
        out_specs=_ROW_SPEC,
        out_shape=jax.ShapeDtypeStruct((n_steps * ROWS, D_MODEL), F32),
        scratch_shapes=_FFN_SCRATCH,
        compiler_params=_FFN_PARAMS,
        name="ffn_first",
    )(x_prompt, x_sample, meta, lnw, wg, wu, wd)


def _ffn_last(h, layer, lnw, wg, wu, wd, lnf):
    n_steps = h.shape[0] // ROWS
    n_seq_blocks = n_steps - 2
    return pl.pallas_call(
        functools.partial(_ffn_last_kernel, layer=layer),
        grid=(n_steps,),
        in_specs=[_ROW_SPEC] + _FFN_WEIGHT_SPECS + [_const_spec((1, D_MODEL))],
        out_specs=[_prompt_block_spec(n_seq_blocks),
                   pl.BlockSpec((N_STREAMS, BLK, D_MODEL), lambda i: (0, 0, 0))],
        out_shape=[jax.ShapeDtypeStruct((N_STREAMS, n_seq_blocks * BLK, D_MODEL), F32),
                   jax.ShapeDtypeStruct((N_STREAMS, BLK, D_MODEL), F32)],
        scratch_shapes=_FFN_SCRATCH,
        compiler_params=_FFN_PARAMS,
        name="ffn_last",
    )(h, lnw, wg, wu, wd, lnf)


def _mixer_kernel(h_ref, lnw_ref, win_ref, wx_ref, wdt_ref, convw_ref, convb_ref, dtb_ref, alog_ref,
                  dskip_ref, ssmnw_ref, lbraw_ref, hgnw_ref, wout_ref, bmat_ref,
                  sconv_ref, sssm_ref, shg_ref,
                  hout_ref, oconv_ref, ossm_ref, ohg_ref,
                  proj_ref, xbc_ref, dtraw_ref, mix_ref, gb_ref, *, layer, n_prompt_blocks):
    step = pl.program_id(0)

    @pl.when(step == 0)
    def _():
        oconv_ref[...] = jnp.zeros(oconv_ref.shape, F32)
        ossm_ref[...] = jnp.zeros(ossm_ref.shape, F32)
        ohg_ref[...] = jnp.zeros(ohg_ref.shape, F32)

    @pl.when(step == n_prompt_blocks)
    def _():
        oconv_ref[0] = sconv_ref[...]
        ossm_ref[0] = sssm_ref[...]
        ohg_ref[0] = shg_ref[...]

    ready = [set() for _ in range(N_PARTS)]

    def in_projection(part):
        r0 = part * PART_ROWS
        rs = slice(r0, r0 + PART_ROWS)
        hn = _rms(h_ref[rs, :], lnw_ref[...]).astype(BF16)
        yield
        for c0 in range(0, CONV_CH, PROJ_CHUNK):
            x_new = _dot(hn, wx_ref[:, c0:c0 + PROJ_CHUNK])
            for k in range(STREAMS_PER_PART):
                for lt in range(PROJ_CHUNK // LANES):
                    xbc_ref[part * STREAMS_PER_PART + k, c0 // LANES + lt, SUB:SUB + BLK, :] = (
                        x_new[k * BLK:(k + 1) * BLK, lt * LANES:(lt + 1) * LANES])
            yield
        ready[part].add("x")
        for c0 in range(0, SSM_WIDTH, PROJ_CHUNK):
            dtraw_ref[rs, c0:c0 + PROJ_CHUNK] = _dot(hn, wdt_ref[:, c0:c0 + PROJ_CHUNK])
            yield
        ready[part].add("dt")
        for name, c_lo in (("z", C_Z), ("q", C_Q), ("f", C_F), ("i", C_I), ("g", C_G)):
            for c0 in range(c_lo, c_lo + HG_WIDTH, PROJ_CHUNK):
                proj_ref[rs, c0:c0 + PROJ_CHUNK] = _dot(hn, win_ref[:, c0:c0 + PROJ_CHUNK])
                yield
            ready[part].add(name)

    def out_projection(part):
        r0 = part * PART_ROWS
        rs = slice(r0, r0 + PART_ROWS)
        for c0 in range(0, D_MODEL, PROJ_CHUNK):
            cs = slice(c0, c0 + PROJ_CHUNK)
            hout_ref[rs, cs] = h_ref[rs, cs] + _dot(mix_ref[rs, :], wout_ref[:, cs])
            yield

    first_valid = jnp.where(step == 0, N_PAD, 0)
    t_col = lax.broadcasted_iota(jnp.int32, (BLK, 1), 0)
    valid = t_col >= first_valid

    tri3 = (lax.broadcasted_iota(jnp.int32, (BLK, 3 * BLK), 0)
            >= lax.broadcasted_iota(jnp.int32, (BLK, 3 * BLK), 1) % BLK).astype(BF16)

    conv_w = convw_ref[...]
    conv_b = convb_ref[...]
    a2 = -jnp.exp(alog_ref[...]) * LOG2E
    row_w = lax.broadcasted_iota(jnp.int32, (BLK, SSM_WIDTH), 0)
    col_w = lax.broadcasted_iota(jnp.int32, (BLK, SSM_WIDTH), 1)
    diag_pick = (col_w % BLK) == row_w
    row_g = lax.broadcasted_iota(jnp.int32, (BLK, SSM_GROUP_WIDTH), 0)
    col_g = lax.broadcasted_iota(jnp.int32, (BLK, SSM_GROUP_WIDTH), 1)
    causal_g = (col_g % BLK) <= row_g
    bd_r = lax.broadcasted_iota(jnp.int32, (SSM_GROUP_WIDTH, SSM_GROUP_WIDTH), 0)
    bd_c = lax.broadcasted_iota(jnp.int32, (SSM_GROUP_WIDTH, SSM_GROUP_WIDTH), 1)
    head_diag = ((bd_r // BLK) == (bd_c // BLK)).astype(BF16)

    raw = lbraw_ref[...]
    e_raw = jnp.exp(raw - jnp.max(raw, axis=0, keepdims=True))
    sm = e_raw / jnp.sum(e_raw, axis=0, keepdims=True)
    lb = jnp.clip(jnp.sum(sm[0:layer + 1], axis=0, keepdims=True) - sm[0:1], 0.0, 1.0 - 1e-6)
    log_lb = jnp.log(jnp.maximum(lb, LB_FLOOR))
    log_1m_lb = jnp.log1p(-lb)
    big_rows = jnp.where(lax.broadcasted_iota(jnp.int32, (BLK, HG_WIDTH), 0) == 0, BIG, 0.0).astype(BF16)
    piece_col = lax.broadcasted_iota(jnp.int32, (SUB, BLK), 1)

    def per_stream(b, slot):
        r0 = b * BLK if isinstance(b, int) else pl.multiple_of(b * BLK, BLK)
        rows = pl.ds(r0, BLK)
        projected = ready[b // STREAMS_PER_PART]

        def wait_for(name):
            while name not in projected:
                yield

        yield from wait_for("x")
        tail = oconv_ref[0, b]
        conv_parts = []
        for lt in range(CONV_TILES):
            ls = slice(lt * LANES, (lt + 1) * LANES)
            xbc_ref[b, lt, 0:SUB, :] = tail[:, ls]
            part = conv_b[:, ls]
            for s in range(CONV_WIDTH - 1, -1, -1):
                tap = xbc_ref[b, lt, pl.ds(SUB - s, BLK, stride=1), :]
                part = part + tap * conv_w[CONV_WIDTH - 1 - s:CONV_WIDTH - s, ls]
            conv_parts.append(part)
        oconv_ref[0, b] = jnp.concatenate(
            [xbc_ref[b, lt, BLK:BLK + SUB, :] for lt in range(CONV_TILES)], axis=1)
        xbc = _silu(jnp.concatenate(conv_parts, axis=1))
        xs = xbc[:, 0:SSM_WIDTH]
        yield

        yield from wait_for("dt")
        dt = _softplus(dtraw_ref[rows, :] + dtb_ref[...])
        dt = jnp.where(valid, dt, 0.0)
        decay_pieces = _split3(dt * a2)
        groups = [slice(g * SSM_GROUP_WIDTH, (g + 1) * SSM_GROUP_WIDTH) for g in range(SSM_GROUPS)]
        b16 = [xbc[:, SSM_WIDTH + g * SSM_STATE:SSM_WIDTH + (g + 1) * SSM_STATE].astype(BF16)
               for g in range(SSM_GROUPS)]
        c16 = [xbc[:, SSM_WIDTH + (SSM_GROUPS + g) * SSM_STATE:
                   SSM_WIDTH + (SSM_GROUPS + g + 1) * SSM_STATE].astype(BF16) for g in range(SSM_GROUPS)]
        xdt = [xs[:, cs] * dt[:, cs] for cs in groups]
        xdt_bd = [jnp.concatenate([x.astype(BF16)] * 4, axis=0) * head_diag for x in xdt]
        st16 = [ossm_ref[0, b, g].astype(BF16) for g in range(SSM_GROUPS)]
        yield
        cum = _dot(tri3, decay_pieces)
        cbm = [_dot_nt(c16[g], jnp.concatenate([b16[g]] * 4, axis=0)) for g in range(SSM_GROUPS)]
        y_state = [_dot(c16[g], st16[g]) for g in range(SSM_GROUPS)]
        yield
        cum_row = jnp.sum(jnp.where(diag_pick, cum, 0.0), axis=0, keepdims=True)
        att16, xdt_w, last = [], [], []
        for g, cs in enumerate(groups):
            decay = jnp.where(causal_g, jnp.exp2(cum[:, cs] - cum_row[:, cs]), 0.0)
            att16.append((cbm[g] * decay).astype(BF16))
            last.append(cum[BLK - 1:BLK, cs])
            xdt_w.append((xdt[g] * jnp.exp2(last[g] - cum[:, cs])).astype(BF16))
        yield
        y_intra = [_dot(att16[g], xdt_bd[g]) for g in range(SSM_GROUPS)]
        st_new = [_dot_tn(b16[g], xdt_w[g]) for g in range(SSM_GROUPS)]
        yield
        yield from wait_for("z")
        y_parts = []
        for g, cs in enumerate(groups):
            ossm_ref[0, b, g] = jnp.exp2(last[g]) * ossm_ref[0, b, g] + st_new[g]
            y_g = y_intra[g] + y_state[g] * jnp.exp2(cum[:, cs]) + dskip_ref[:, cs] * xs[:, cs]
            y_g = y_g * _silu(proj_ref[rows, C_Z + g * SSM_GROUP_WIDTH:C_Z + (g + 1) * SSM_GROUP_WIDTH])
            y_g = y_g * lax.rsqrt(jnp.mean(y_g * y_g, axis=-1, keepdims=True) + EPS)
            y_parts.append(y_g * ssmnw_ref[:, cs])
        mix_ref[rows, 0:SSM_WIDTH] = jnp.concatenate(y_parts, axis=1).astype(BF16)
        yield

        yield from wait_for("f")
        q = _silu(proj_ref[rows, C_Q:C_Q + HG_WIDTH])
        fr = proj_ref[rows, C_F:C_F + HG_WIDTH]
        log_sig = jnp.minimum(fr, 0.0) - jnp.log(1.0 + jnp.exp(-jnp.abs(fr)))
        t2 = log_1m_lb + log_sig
        lf = jnp.maximum(log_lb, t2) + jnp.log(1.0 + jnp.exp(-jnp.abs(log_lb - t2)))
        lf = jnp.where(valid, lf, 0.0)
        log2_key = jnp.where(valid, (t2 - fr) * LOG2E, -BIG)
        forget_pieces = _split3(lf * LOG2E)
        yield
        cumh = _dot(tri3, forget_pieces)
        yield
        g = cumh - log2_key
        lasth = cumh[BLK - 1:BLK]
        q_in = (q * jnp.exp2(cumh)).astype(BF16)
        k_out = jnp.exp2(lasth - g).astype(BF16)
        g_pieces = jnp.concatenate([_split3(g), big_rows], axis=0)

        q_cols, k_cols = [], []
        for s in LEVELS:
            for m in range(0, BLK, 2 * s):
                lo, up = slice(m, m + s), slice(m + s, m + 2 * s)
                mid = cumh[m + s - 1:m + s]
                k_cols.append(_embed_rows(jnp.exp2(mid - g[lo]), m))
                q_cols.append(_embed_rows(q[up] * jnp.exp2(cumh[up] - mid), m + s))
        yield
        gb_ref[slot] = _dot(bmat_ref[...], g_pieces)
        att_off = []
        for hh in range(HG_HEADS):
            hs = slice(hh * HG_DIM, (hh + 1) * HG_DIM)
            att_off.append(_dot_nt(jnp.concatenate([c[:, hs] for c in q_cols], axis=1),
                                   jnp.concatenate([c[:, hs] for c in k_cols], axis=1)))
        yield
        diag_pieces = []
        for u in range(BLK // SUB):
            rs = slice(u * SUB, (u + 1) * SUB)
            pieces = [jnp.zeros((SUB, BLK), F32) for _ in range(HG_HEADS)]
            for j in range(SUB):
                g_j = gb_ref[slot, u * BLK + j * SUB:u * BLK + (j + 1) * SUB, :]
                slab = q[rs] * jnp.exp2(cumh[rs] - g_j)
                for hh in range(HG_HEADS):
                    pair_sum = jnp.sum(slab[:, hh * HG_DIM:(hh + 1) * HG_DIM], axis=-1, keepdims=True)
                    pieces[hh] = jnp.where(piece_col == u * SUB + j, pair_sum, pieces[hh])
            diag_pieces.append(pieces)
            if u % 2:
                yield
        yield from wait_for("i")
        v16 = proj_ref[rows, C_I:C_I + HG_WIDTH].astype(BF16)
        decay_rows = jnp.exp2(lasth)
        heads = [slice(hh * HG_DIM, (hh + 1) * HG_DIM) for hh in range(HG_HEADS)]
        lhs, rhs, decay_cols = [], [], []
        for hh, hs in enumerate(heads):
            att_d = jnp.concatenate([diag_pieces[u][hh] for u in range(BLK // SUB)], axis=0)
            a_h = (att_off[hh] + att_d).astype(BF16)
            lhs.append(jnp.concatenate([q_in[:, hs], a_h], axis=1))
            rhs.append(jnp.concatenate([ohg_ref[0, b, hh].astype(BF16), v16[:, hs]], axis=0))
            decay_cols.append(jnp.broadcast_to(decay_rows[:, hs], (HG_DIM, HG_DIM)).T)
        yield
        o_parts = [_dot(lhs[hh], rhs[hh]) for hh in range(HG_HEADS)]
        kv_new = [_dot_tn(k_out[:, hs], v16[:, hs]) for hs in heads]
        yield
        for hh in range(HG_HEADS):
            ohg_ref[0, b, hh] = ohg_ref[0, b, hh] * decay_cols[hh] + kv_new[hh]
        yield from wait_for("g")
        o_parts = [o_h * lax.rsqrt(jnp.mean(o_h * o_h, axis=-1, keepdims=True) + EPS) for o_h in o_parts]
        o = jnp.concatenate(o_parts, axis=1) * hgnw_ref[...]
        o = o * _silu(proj_ref[rows, C_G:C_G + HG_WIDTH])
        mix_ref[rows, SSM_WIDTH:SSM_WIDTH + HG_WIDTH] = o.astype(BF16)

    tasks = {}
    for part in range(N_PARTS):
        first = part * STREAMS_PER_PART
        streams = ["s%d" % b for b in range(first, first + STREAMS_PER_PART)]
        tasks["in%d" % part] = (in_projection(part), [] if part == 0 else ["in%d" % (part - 1)])
        for b in range(first, first + STREAMS_PER_PART):
            tasks["s%d" % b] = (per_stream(b, b % SLAB_SLOTS), [])
        tasks["out%d" % part] = (out_projection(part), streams)
    done, active, rnd, next_stream_start = set(), [], 0, 0
    pending = list(tasks)
    while pending or active:
        for name in list(pending):
            is_stream = name.startswith("s")
            if all(dep in done for dep in tasks[name][1]) and not (is_stream and rnd < next_stream_start):
                pending.remove(name)
                active.append(name)
                if is_stream:
                    next_stream_start = rnd + STAGGER[int(name[1:]) // STREAMS_PER_PART]
        for name in list(active):
            if next(tasks[name][0], True) is not None:
                active.remove(name)
                done.add(name)
        rnd += 1


def _row_broadcast_matrix():
    r = np.arange(SUB * BLK)
    u, j, i = r // BLK, (r % BLK) // SUB, r % SUB
    c = np.arange(4 * BLK)
    part, t = c // BLK, c % BLK
    pick = (part[None, :] < 3) & (t[None, :] == (u * SUB + j)[:, None])
    mask = (part[None, :] == 3) & (t[None, :] == 0) & (i < j)[:, None]
    return jnp.asarray(pick | mask, dtype=BF16)


def _mixer(h, p, layer, n_prompt_blocks, sconv, sssm, shg):
    n_steps = h.shape[0] // ROWS
    row_spec = pl.BlockSpec((ROWS, D_MODEL), lambda i: (i, 0))
    group = lambda i: i // n_prompt_blocks
    conv_shape = (2, N_STREAMS, SUB, CONV_CH)
    ssm_shape = (2, N_STREAMS, SSM_GROUPS, SSM_STATE, SSM_GROUP_WIDTH)
    hg_shape = (2, N_STREAMS, HG_HEADS, HG_DIM, HG_DIM)
    in_specs = [
        row_spec,
        _const_spec((1, D_MODEL)),
        _layer_spec((D_MODEL, IN_MAIN), layer),
        _layer_spec((D_MODEL, CONV_CH), layer),
        _layer_spec((D_MODEL, SSM_WIDTH), layer),
        _const_spec((CONV_WIDTH, CONV_CH)),
        _const_spec((1, CONV_CH)),
        _const_spec((1, SSM_WIDTH)),
        _const_spec((1, SSM_WIDTH)),
        _const_spec((1, SSM_WIDTH)),
        _const_spec((1, SSM_WIDTH)),
        _const_spec((p["lb_raw"].shape[0], HG_WIDTH)),
        _const_spec((1, HG_WIDTH)),
        _layer_spec((D_MODEL, D_MODEL), layer),
        _const_spec((SUB * BLK, 4 * BLK)),
        _layer_spec(conv_shape[1:], layer),
        _layer_spec(ssm_shape[1:], layer),
        _layer_spec(hg_shape[1:], layer),
    ]
    out_specs = [
        row_spec,
        pl.BlockSpec((1,) + conv_shape[1:], lambda i: (group(i), 0, 0, 0)),
        pl.BlockSpec((1,) + ssm_shape[1:], lambda i: (group(i), 0, 0, 0, 0)),
        pl.BlockSpec((1,) + hg_shape[1:], lambda i: (group(i), 0, 0, 0, 0)),
    ]
    out_shape = [
        jax.ShapeDtypeStruct(h.shape, F32),
        jax.ShapeDtypeStruct(conv_shape, F32),
        jax.ShapeDtypeStruct(ssm_shape, F32),
        jax.ShapeDtypeStruct(hg_shape, F32),
    ]
    return pl.pallas_call(
        functools.partial(_mixer_kernel, layer=layer, n_prompt_blocks=n_prompt_blocks),
        grid=(n_steps,),
        in_specs=in_specs,
        out_specs=out_specs,
        out_shape=out_shape,
        scratch_shapes=[
            pltpu.VMEM((ROWS, IN_MAIN), F32),
            pltpu.VMEM((N_STREAMS, CONV_TILES, SUB + BLK, LANES), F32),
            pltpu.VMEM((ROWS, SSM_WIDTH), F32),
            pltpu.VMEM((ROWS, D_MODEL), BF16),
            pltpu.VMEM((SLAB_SLOTS, SUB * BLK, HG_WIDTH), F32),
        ],
        compiler_params=pltpu.CompilerParams(
            dimension_semantics=("arbitrary",), vmem_limit_bytes=VMEM_LIMIT_BYTES),
        name="mixer",
    )(h, p["ln_mix"], p["w_in"], p["w_x"], p["w_dt"], p["conv_w"], p["conv_b"], p["dt_bias"], p["a_log"],
      p["d_skip"], p["ssm_norm"], p["lb_raw"], p["hg_norm"], p["w_out"],
      _row_broadcast_matrix(), sconv, sssm, shg)


def _expand_heads(x):
    return jnp.repeat(x.astype(F32), SSM_HEADDIM)[None, :]


def kernel(x_prompt, x_sample, state_conv, state_ssm, state_hgrn, meta_tokens, ln_ffa_w, ffa_w_gate, ffa_w_up, ffa_w_down, ln_mix_w, w_in, conv_w, conv_b, dt_bias, a_log, d_skip, ssm_norm_w, hg_lb_raw, hg_norm_w, w_out, ln_ffb_w, ffb_w_gate, ffb_w_up, ffb_w_down, ln_f_w):
    depth = w_in.shape[0]
    nb, seq, _ = x_prompt.shape
    assert nb == N_STREAMS and x_sample.shape[:2] == (N_STREAMS, BLK) and seq % BLK == 0
    n_seq_blocks = seq // BLK
    n_prompt_blocks = n_seq_blocks + 1

    row = lambda a: a.astype(F32)[None, :]
    dt_cols = slice(SSM_WIDTH + CONV_CH, SSM_WIDTH + CONV_CH + SSM_WIDTH // SSM_HEADDIM)
    w_main16 = jnp.concatenate([w_in[:, :, :SSM_WIDTH], w_in[:, :, dt_cols.stop:]], axis=2).astype(BF16)
    w_x16 = w_in[:, :, SSM_WIDTH:dt_cols.start].astype(BF16)
    w_dt16 = jnp.repeat(w_in[:, :, dt_cols], SSM_HEADDIM, axis=2).astype(BF16)
    w_out16 = w_out.astype(BF16)
    ffa16 = (ffa_w_gate.astype(F32), ffa_w_up.astype(F32), ffa_w_down.astype(F32))
    ffb16 = (ffb_w_gate.astype(F32), ffb_w_up.astype(F32), ffb_w_down.astype(F32))
    sconv_all = jnp.pad(state_conv.astype(F32), ((0, 0), (0, 0), (SUB - (CONV_WIDTH - 1), 0), (0, 0)))
    sssm_all = (state_ssm.astype(F32)
                .reshape(depth, nb, SSM_GROUPS, SSM_GROUP_WIDTH // SSM_HEADDIM, SSM_HEADDIM, SSM_STATE)
                .transpose(0, 1, 2, 5, 3, 4).reshape(depth, nb, SSM_GROUPS, SSM_STATE, SSM_GROUP_WIDTH))
    shg_all = state_hgrn.astype(F32)

    conv_out, ssm_out, hg_out = [], [], []
    for l in range(depth):
        params = {
            "ln_mix": row(ln_mix_w[l]),
            "w_in": w_main16,
            "w_x": w_x16,
            "w_dt": w_dt16,
            "conv_w": conv_w[l].astype(F32),
            "conv_b": row(conv_b[l]),
            "dt_bias": _expand_heads(dt_bias[l]),
            "a_log": _expand_heads(a_log[l]),
            "d_skip": _expand_heads(d_skip[l]),
            "ssm_norm": row(ssm_norm_w[l]),
            "lb_raw": hg_lb_raw.astype(F32),
            "hg_norm": row(hg_norm_w[l]),
            "w_out": w_out16,
        }
        if l == 0:
            h = _ffn_first(x_prompt.astype(F32), x_sample.astype(F32), meta_tokens.astype(F32),
                           l, row(ln_ffa_w[l]), *ffa16)
        else:
            h = _ffn(h, l, row(ln_ffa_w[l]), *ffa16)
        h, oc, os_, og = _mixer(h, params, l, n_prompt_blocks, sconv_all, sssm_all, shg_all)
        if l == depth - 1:
            y_prompt, y_sample = _ffn_last(h, l, row(ln_ffb_w[l]), *ffb16, row(ln_f_w))
        else:
            h = _ffn(h, l, row(ln_ffb_w[l]), *ffb16)
        conv_out.append(oc)
        ssm_out.append(os_)
        hg_out.append(og)

    conv_all = jnp.stack(conv_out)[:, :, :, SUB - (CONV_WIDTH - 1):, :]
    ssm_all = (jnp.stack(ssm_out)
               .reshape(depth, 2, nb, SSM_GROUPS, SSM_STATE, SSM_GROUP_WIDTH // SSM_HEADDIM, SSM_HEADDIM)
               .transpose(0, 1, 2, 3, 5, 6, 4)
               .reshape(depth, 2, nb, SSM_WIDTH // SSM_HEADDIM, SSM_HEADDIM, SSM_STATE))
    hg_all = jnp.stack(hg_out)
    dt_ = x_prompt.dtype
    return (y_prompt.astype(dt_), y_sample.astype(x_sample.dtype),
            conv_all[:, 0].astype(dt_), ssm_all[:, 0].astype(dt_), hg_all[:, 0].astype(dt_),
            conv_all[:, 1].astype(state_conv.dtype), ssm_all[:, 1].astype(state_ssm.dtype),
            hg_all[:, 1].astype(state_hgrn.dtype))
```

```python
import functools

import numpy as np
import jax
import jax.numpy as jnp
from jax import lax
from jax.experimental import pallas as pl
from jax.experimental.pallas import tpu as pltpu

F32 = jnp.float32
BF16 = jnp.bfloat16

D_MODEL = 1024
N_STREAMS = 8
BLK = 64
ROWS = N_STREAMS * BLK
N_META = 16
N_PAD = BLK - N_META
SSM_WIDTH = 512
SSM_GROUPS = 2
SSM_GROUP_WIDTH = SSM_WIDTH // SSM_GROUPS
SSM_HEADDIM = 64
SSM_STATE = 128
CONV_CH = 1024
CONV_WIDTH = 4
HG_WIDTH = 512
HG_HEADS = 4
HG_DIM = 128
D_FF = 2816
MXU_TILE = 256
FF_CHUNK_EDGES = (0, 6 * MXU_TILE, D_FF)
EPS = 1e-6
LB_FLOOR = 1e-30
SUB = 8
LEVELS = (32, 16, 8)
LOG2E = 1.4426950408889634

N_PARTS = 2
STREAMS_PER_PART = N_STREAMS // N_PARTS
PART_ROWS = STREAMS_PER_PART * BLK
PROJ_CHUNK = 256
STREAM_LAG = 1
PREP_LEAD = 2
SLAB_SLOTS = 4
STAGGER = (1, 1)
LANES = 128
CONV_TILES = CONV_CH // LANES
BIG = float(2 ** 100)

C_Z, C_Q, C_F, C_I, C_G = 0, 512, 1024, 1536, 2048
IN_MAIN = 2560

VMEM_LIMIT_BYTES = 56 * 1024 * 1024


def _rms(x, w):
    return x * lax.rsqrt(jnp.mean(x * x, axis=-1, keepdims=True) + EPS) * w


def _silu(x):
    return x / (1.0 + jnp.exp(-x))


def _log1p(y):
    u = 1.0 + y
    return jnp.where(u == 1.0, y, jnp.log(u) * (y / (u - 1.0)))


def _softplus(x):
    return jnp.maximum(x, 0.0) + _log1p(jnp.exp(-jnp.abs(x)))


def _dot(a, b):
    return jnp.dot(a, b, preferred_element_type=F32)


def _dot_nt(a, b):
    return lax.dot_general(a, b, (((1,), (1,)), ((), ())), preferred_element_type=F32)


def _dot_tn(a, b):
    return lax.dot_general(a, b, (((0,), (0,)), ((), ())), preferred_element_type=F32)


def _embed_rows(piece, r0):
    s, w = piece.shape
    tile = 2 * SUB
    if s % tile:
        assert s == SUB and r0 % SUB == 0
        zeros = jnp.zeros((SUB, w), F32)
        piece = jnp.concatenate([zeros, piece] if r0 % tile else [piece, zeros], axis=0)
        r0, s = r0 - r0 % tile, tile
    assert r0 % tile == 0 and s % tile == 0
    parts = [piece.astype(BF16)]
    if r0:
        parts.insert(0, jnp.zeros((r0, w), BF16))
    if BLK - r0 - s:
        parts.append(jnp.zeros((BLK - r0 - s, w), BF16))
    return jnp.concatenate(parts, axis=0) if len(parts) > 1 else parts[0]


def _stages(n):
    for _ in range(n):
        yield


def _split3(x):
    hi = x.astype(BF16)
    r1 = x - hi.astype(F32)
    mid = r1.astype(BF16)
    lo = (r1 - mid.astype(F32)).astype(BF16)
    return jnp.concatenate([hi, mid, lo], axis=0)


def _swiglu_step(h, lnw_ref, wg_ref, wu_ref, wd_ref):
    hn = _rms(h, lnw_ref[...]).astype(BF16)
    acc = jnp.zeros_like(h)
    for c0, c1 in zip(FF_CHUNK_EDGES[:-1], FF_CHUNK_EDGES[1:]):
        cs = slice(c0, c1)
        g = _dot(hn, wg_ref[:, cs])
        u = _dot(hn, wu_ref[:, cs])
        a = (_silu(g) * u).astype(BF16)
        acc = acc + _dot(a, wd_ref[cs, :])
    return h + 0.5 * acc


def _fetch_weight_bf16(w_hbm, layer, w16_ref, stage_ref, sem):
    rows = stage_ref.shape[1]
    n_chunks = w16_ref.shape[0] // rows

    def chunk_copy(k):
        return pltpu.make_async_copy(w_hbm.at[layer, pl.ds(k * rows, rows), :],
                                     stage_ref.at[k % 2], sem.at[k % 2])

    chunk_copy(0).start()
    for k in range(n_chunks):
        if k + 1 < n_chunks:
            chunk_copy(k + 1).start()
        chunk_copy(k).wait()
        w16_ref[k * rows:(k + 1) * rows, :] = stage_ref[k % 2].astype(BF16)


def _fetch_ffn_weights(layer, wg_hbm, wu_hbm, wd_hbm, wg_ref, wu_ref, wd_ref, wide_stage, wide_sem,
                       narrow_stage, narrow_sem):
    @pl.when(pl.program_id(0) == 0)
    def _():
        _fetch_weight_bf16(wg_hbm, layer, wg_ref, wide_stage, wide_sem)
        _fetch_weight_bf16(wu_hbm, layer, wu_ref, wide_stage, wide_sem)
        _fetch_weight_bf16(wd_hbm, layer, wd_ref, narrow_stage, narrow_sem)


def _ffn_kernel(h_ref, lnw_ref, wg_hbm, wu_hbm, wd_hbm, o_ref, wg_ref, wu_ref, wd_ref, *stage, layer):
    _fetch_ffn_weights(layer, wg_hbm, wu_hbm, wd_hbm, wg_ref, wu_ref, wd_ref, *stage)
    o_ref[...] = _swiglu_step(h_ref[...], lnw_ref, wg_ref, wu_ref, wd_ref)


def _ffn_first_kernel(xp_ref, xs_ref, meta_ref, lnw_ref, wg_hbm, wu_hbm, wd_hbm, o_ref,
                      wg_ref, wu_ref, wd_ref, *stage, layer):
    _fetch_ffn_weights(layer, wg_hbm, wu_hbm, wd_hbm, wg_ref, wu_ref, wd_ref, *stage)
    step = pl.program_id(0)
    first = jnp.concatenate([jnp.zeros((N_PAD, D_MODEL), F32), meta_ref[...]], axis=0)
    h = jnp.where(step == 0, first[None], xp_ref[...])
    h = jnp.where(step == pl.num_programs(0) - 1, xs_ref[...], h)
    o_ref[...] = _swiglu_step(h.reshape(ROWS, D_MODEL), lnw_ref, wg_ref, wu_ref, wd_ref)


def _ffn_last_kernel(h_ref, lnw_ref, wg_hbm, wu_hbm, wd_hbm, lnf_ref, yp_ref, ys_ref,
                     wg_ref, wu_ref, wd_ref, *stage, layer):
    _fetch_ffn_weights(layer, wg_hbm, wu_hbm, wd_hbm, wg_ref, wu_ref, wd_ref, *stage)
    step = pl.program_id(0)
    last = pl.num_programs(0) - 1
    y = _rms(_swiglu_step(h_ref[...], lnw_ref, wg_ref, wu_ref, wd_ref), lnf_ref[...])
    y = y.reshape(N_STREAMS, BLK, D_MODEL)

    @pl.when(step < last)
    def _():
        yp_ref[...] = y

    @pl.when(step == last)
    def _():
        ys_ref[...] = y


def _const_spec(shape):
    nd = len(shape)
    return pl.BlockSpec(shape, lambda i: (0,) * nd, pipeline_mode=pl.Buffered(1))


def _layer_spec(shape, layer):
    nd = len(shape)
    return pl.BlockSpec((None,) + shape, lambda i: (layer,) + (0,) * nd, pipeline_mode=pl.Buffered(1))


_HBM_SPEC = pl.BlockSpec(memory_space=pl.ANY)
_FFN_WEIGHT_SPECS = [_const_spec((1, D_MODEL)), _HBM_SPEC, _HBM_SPEC, _HBM_SPEC]
FETCH_ROWS = MXU_TILE
_FFN_SCRATCH = [
    pltpu.VMEM((D_MODEL, D_FF), BF16), pltpu.VMEM((D_MODEL, D_FF), BF16), pltpu.VMEM((D_FF, D_MODEL), BF16),
    pltpu.VMEM((2, FETCH_ROWS, D_FF), F32), pltpu.SemaphoreType.DMA((2,)),
    pltpu.VMEM((2, FETCH_ROWS, D_MODEL), F32), pltpu.SemaphoreType.DMA((2,)),
]
_ROW_SPEC = pl.BlockSpec((ROWS, D_MODEL), lambda i: (i, 0))
_FFN_PARAMS = pltpu.CompilerParams(dimension_semantics=("arbitrary",), vmem_limit_bytes=VMEM_LIMIT_BYTES)


def _prompt_block_spec(n_seq_blocks):
    return pl.BlockSpec((N_STREAMS, BLK, D_MODEL),
                        lambda i: (0, jnp.clip(i - 1, 0, n_seq_blocks - 1), 0))


def _ffn(h, layer, lnw, wg, wu, wd):
    return pl.pallas_call(
        functools.partial(_ffn_kernel, layer=layer),
        grid=(h.shape[0] // ROWS,),
        in_specs=[_ROW_SPEC] + _FFN_WEIGHT_SPECS,
        out_specs=_ROW_SPEC,
        out_shape=jax.ShapeDtypeStruct(h.shape, F32),
        scratch_shapes=_FFN_SCRATCH,
        compiler_params=_FFN_PARAMS,
        name="ffn",
    )(h, lnw, wg, wu, wd)


def _ffn_first(x_prompt, x_sample, meta, layer, lnw, wg, wu, wd):
    n_seq_blocks = x_prompt.shape[1] // BLK
    n_steps = n_seq_blocks + 2
    return pl.pallas_call(
        functools.partial(_ffn_first_kernel, layer=layer),
        grid=(n_steps,),
        in_specs=[_prompt_block_spec(n_seq_blocks), _const_spec((N_STREAMS, BLK, D_MODEL)),
                  _const_spec((N_META, D_MODEL))] + _FFN_WEIGHT_SPECS,
        out_specs=_ROW_SPEC,
        out_shape=jax.ShapeDtypeStruct((n_steps * ROWS, D_MODEL), F32),
        scratch_shapes=_FFN_SCRATCH,
        compiler_params=_FFN_PARAMS,
        name="ffn_first",
    )(x_prompt, x_sample, meta, lnw, wg, wu, wd)


def _ffn_last(h, layer, lnw, wg, wu, wd, lnf):
    n_steps = h.shape[0] // ROWS
    n_seq_blocks = n_steps - 2
    return pl.pallas_call(
        functools.partial(_ffn_last_kernel, layer=layer),
        grid=(n_steps,),
        in_specs=[_ROW_SPEC] + _FFN_WEIGHT_SPECS + [_const_spec((1, D_MODEL))],
        out_specs=[_prompt_block_spec(n_seq_blocks),
                   pl.BlockSpec((N_STREAMS, BLK, D_MODEL), lambda i: (0, 0, 0))],
        out_shape=[jax.ShapeDtypeStruct((N_STREAMS, n_seq_blocks * BLK, D_MODEL), F32),
                   jax.ShapeDtypeStruct((N_STREAMS, BLK, D_MODEL), F32)],
        scratch_shapes=_FFN_SCRATCH,
        compiler_params=_FFN_PARAMS,
        name="ffn_last",
    )(h, lnw, wg, wu, wd, lnf)


def _mixer_kernel(h_ref, lnw_ref, win_ref, wx_ref, wdt_ref, convw_ref, convb_ref, dtb_ref, alog_ref,
                  dskip_ref, ssmnw_ref, lbraw_ref, hgnw_ref, wout_ref, bmat_ref,
                  sconv_ref, sssm_ref, shg_ref,
                  hout_ref, oconv_ref, ossm_ref, ohg_ref,
                  proj_ref, xbc_ref, dtraw_ref, mix_ref, gb_ref, *, layer, n_prompt_blocks):
    step = pl.program_id(0)

    @pl.when(step == 0)
    def _():
        oconv_ref[...] = jnp.zeros(oconv_ref.shape, F32)
        ossm_ref[...] = jnp.zeros(ossm_ref.shape, F32)
        ohg_ref[...] = jnp.zeros(ohg_ref.shape, F32)

    @pl.when(step == n_prompt_blocks)
    def _():
        oconv_ref[0] = sconv_ref[...]
        ossm_ref[0] = sssm_ref[...]
        ohg_ref[0] = shg_ref[...]

    ready = [set() for _ in range(N_PARTS)]

    def in_projection(part):
        r0 = part * PART_ROWS
        rs = slice(r0, r0 + PART_ROWS)
        hn = _rms(h_ref[rs, :], lnw_ref[...]).astype(BF16)
        yield
        for c0 in range(0, CONV_CH, PROJ_CHUNK):
            x_new = _dot(hn, wx_ref[:, c0:c0 + PROJ_CHUNK])
            for k in range(STREAMS_PER_PART):
                for lt in range(PROJ_CHUNK // LANES):
                    xbc_ref[part * STREAMS_PER_PART + k, c0 // LANES + lt, SUB:SUB + BLK, :] = (
                        x_new[k * BLK:(k + 1) * BLK, lt * LANES:(lt + 1) * LANES])
            yield
        ready[part].add("x")
        for c0 in range(0, SSM_WIDTH, PROJ_CHUNK):
            dtraw_ref[rs, c0:c0 + PROJ_CHUNK] = _dot(hn, wdt_ref[:, c0:c0 + PROJ_CHUNK])
            yield
        ready[part].add("dt")
        for name, c_lo in (("z", C_Z), ("q", C_Q), ("f", C_F), ("i", C_I), ("g", C_G)):
            for c0 in range(c_lo, c_lo + HG_WIDTH, PROJ_CHUNK):
                proj_ref[rs, c0:c0 + PROJ_CHUNK] = _dot(hn, win_ref[:, c0:c0 + PROJ_CHUNK])
                yield
            ready[part].add(name)

    def out_projection(part):
        r0 = part * PART_ROWS
        rs = slice(r0, r0 + PART_ROWS)
        for c0 in range(0, D_MODEL, PROJ_CHUNK):
            cs = slice(c0, c0 + PROJ_CHUNK)
            hout_ref[rs, cs] = h_ref[rs, cs] + _dot(mix_ref[rs, :], wout_ref[:, cs])
            yield

    first_valid = jnp.where(step == 0, N_PAD, 0)
    t_col = lax.broadcasted_iota(jnp.int32, (BLK, 1), 0)
    valid = t_col >= first_valid

    tri3 = (lax.broadcasted_iota(jnp.int32, (BLK, 3 * BLK), 0)
            >= lax.broadcasted_iota(jnp.int32, (BLK, 3 * BLK), 1) % BLK).astype(BF16)

    conv_w = convw_ref[...]
    conv_b = convb_ref[...]
    a2 = -jnp.exp(alog_ref[...]) * LOG2E
    row_w = lax.broadcasted_iota(jnp.int32, (BLK, SSM_WIDTH), 0)
    col_w = lax.broadcasted_iota(jnp.int32, (BLK, SSM_WIDTH), 1)
    diag_pick = (col_w % BLK) == row_w
    row_g = lax.broadcasted_iota(jnp.int32, (BLK, SSM_GROUP_WIDTH), 0)
    col_g = lax.broadcasted_iota(jnp.int32, (BLK, SSM_GROUP_WIDTH), 1)
    causal_g = (col_g % BLK) <= row_g
    bd_r = lax.broadcasted_iota(jnp.int32, (SSM_GROUP_WIDTH, SSM_GROUP_WIDTH), 0)
    bd_c = lax.broadcasted_iota(jnp.int32, (SSM_GROUP_WIDTH, SSM_GROUP_WIDTH), 1)
    head_diag = ((bd_r // BLK) == (bd_c // BLK)).astype(BF16)

    raw = lbraw_ref[...]
    e_raw = jnp.exp(raw - jnp.max(raw, axis=0, keepdims=True))
    sm = e_raw / jnp.sum(e_raw, axis=0, keepdims=True)
    lb = jnp.clip(jnp.sum(sm[0:layer + 1], axis=0, keepdims=True) - sm[0:1], 0.0, 1.0 - 1e-6)
    log_lb = jnp.log(jnp.maximum(lb, LB_FLOOR))
    log_1m_lb = jnp.log1p(-lb)
    big_rows = jnp.where(lax.broadcasted_iota(jnp.int32, (BLK, HG_WIDTH), 0) == 0, BIG, 0.0).astype(BF16)
    piece_col = lax.broadcasted_iota(jnp.int32, (SUB, BLK), 1)

    def per_stream(b, slot):
        r0 = b * BLK if isinstance(b, int) else pl.multiple_of(b * BLK, BLK)
        rows = pl.ds(r0, BLK)
        projected = ready[b // STREAMS_PER_PART]

        def wait_for(name):
            while name not in projected:
                yield

        yield from wait_for("x")
        yield from _stages(STREAM_LAG * (b % STREAMS_PER_PART))
        tail = oconv_ref[0, b]
        conv_parts = []
        for lt in range(CONV_TILES):
            ls = slice(lt * LANES, (lt + 1) * LANES)
            xbc_ref[b, lt, 0:SUB, :] = tail[:, ls]
            part = conv_b[:, ls]
            for s in range(CONV_WIDTH - 1, -1, -1):
                tap = xbc_ref[b, lt, pl.ds(SUB - s, BLK, stride=1), :]
                part = part + tap * conv_w[CONV_WIDTH - 1 - s:CONV_WIDTH - s, ls]
            conv_parts.append(part)
        oconv_ref[0, b] = jnp.concatenate(
            [xbc_ref[b, lt, BLK:BLK + SUB, :] for lt in range(CONV_TILES)], axis=1)
        xbc = _silu(jnp.concatenate(conv_parts, axis=1))
        xs = xbc[:, 0:SSM_WIDTH]
        yield

        yield from wait_for("dt")
        dt = _softplus(dtraw_ref[rows, :] + dtb_ref[...])
        dt = jnp.where(valid, dt, 0.0)
        decay_pieces = _split3(dt * a2)
        groups = [slice(g * SSM_GROUP_WIDTH, (g + 1) * SSM_GROUP_WIDTH) for g in range(SSM_GROUPS)]
        b16 = [xbc[:, SSM_WIDTH + g * SSM_STATE:SSM_WIDTH + (g + 1) * SSM_STATE].astype(BF16)
               for g in range(SSM_GROUPS)]
        c16 = [xbc[:, SSM_WIDTH + (SSM_GROUPS + g) * SSM_STATE:
                   SSM_WIDTH + (SSM_GROUPS + g + 1) * SSM_STATE].astype(BF16) for g in range(SSM_GROUPS)]
        xdt = [xs[:, cs] * dt[:, cs] for cs in groups]
        xdt_bd = [jnp.concatenate([x.astype(BF16)] * 4, axis=0) * head_diag for x in xdt]
        st16 = [ossm_ref[0, b, g].astype(BF16) for g in range(SSM_GROUPS)]
        yield from _stages(PREP_LEAD)
        cum = _dot(tri3, decay_pieces)
        cbm = [_dot_nt(c16[g], jnp.concatenate([b16[g]] * 4, axis=0)) for g in range(SSM_GROUPS)]
        y_state = [_dot(c16[g], st16[g]) for g in range(SSM_GROUPS)]
        yield
        cum_row = jnp.sum(jnp.where(diag_pick, cum, 0.0), axis=0, keepdims=True)
        att16, xdt_w, last = [], [], []
        for g, cs in enumerate(groups):
            decay = jnp.where(causal_g, jnp.exp2(cum[:, cs] - cum_row[:, cs]), 0.0)
            att16.append((cbm[g] * decay).astype(BF16))
            last.append(cum[BLK - 1:BLK, cs])
            xdt_w.append((xdt[g] * jnp.exp2(last[g] - cum[:, cs])).astype(BF16))
        yield from _stages(PREP_LEAD)
        y_intra = [_dot(att16[g], xdt_bd[g]) for g in range(SSM_GROUPS)]
        st_new = [_dot_tn(b16[g], xdt_w[g]) for g in range(SSM_GROUPS)]
        yield
        yield from wait_for("z")
        y_parts = []
        for g, cs in enumerate(groups):
            ossm_ref[0, b, g] = jnp.exp2(last[g]) * ossm_ref[0, b, g] + st_new[g]
            y_g = y_intra[g] + y_state[g] * jnp.exp2(cum[:, cs]) + dskip_ref[:, cs] * xs[:, cs]
            y_g = y_g * _silu(proj_ref[rows, C_Z + g * SSM_GROUP_WIDTH:C_Z + (g + 1) * SSM_GROUP_WIDTH])
            y_g = y_g * lax.rsqrt(jnp.mean(y_g * y_g, axis=-1, keepdims=True) + EPS)
            y_parts.append(y_g * ssmnw_ref[:, cs])
        mix_ref[rows, 0:SSM_WIDTH] = jnp.concatenate(y_parts, axis=1).astype(BF16)
        yield

        yield from wait_for("f")
        q = _silu(proj_ref[rows, C_Q:C_Q + HG_WIDTH])
        fr = proj_ref[rows, C_F:C_F + HG_WIDTH]
        log_sig = jnp.minimum(fr, 0.0) - jnp.log(1.0 + jnp.exp(-jnp.abs(fr)))
        t2 = log_1m_lb + log_sig
        lf = jnp.maximum(log_lb, t2) + jnp.log(1.0 + jnp.exp(-jnp.abs(log_lb - t2)))
        lf = jnp.where(valid, lf, 0.0)
        log2_key = jnp.where(valid, (t2 - fr) * LOG2E, -BIG)
        forget_pieces = _split3(lf * LOG2E)
        yield from _stages(PREP_LEAD)
        cumh = _dot(tri3, forget_pieces)
        yield
        g = cumh - log2_key
        lasth = cumh[BLK - 1:BLK]
        q_in = (q * jnp.exp2(cumh)).astype(BF16)
        k_out = jnp.exp2(lasth - g).astype(BF16)
        g_pieces = jnp.concatenate([_split3(g), big_rows], axis=0)

        q_cols, k_cols = [], []
        for s in LEVELS:
            for m in range(0, BLK, 2 * s):
                lo, up = slice(m, m + s), slice(m + s, m + 2 * s)
                mid = cumh[m + s - 1:m + s]
                k_cols.append(_embed_rows(jnp.exp2(mid - g[lo]), m))
                q_cols.append(_embed_rows(q[up] * jnp.exp2(cumh[up] - mid), m + s))
        yield from _stages(PREP_LEAD)
        gb_ref[slot] = _dot(bmat_ref[...], g_pieces)
        att_off = []
        for hh in range(HG_HEADS):
            hs = slice(hh * HG_DIM, (hh + 1) * HG_DIM)
            att_off.append(_dot_nt(jnp.concatenate([c[:, hs] for c in q_cols], axis=1),
                                   jnp.concatenate([c[:, hs] for c in k_cols], axis=1)))
        yield
        diag_pieces = []
        for u in range(BLK // SUB):
            rs = slice(u * SUB, (u + 1) * SUB)
            pieces = [jnp.zeros((SUB, BLK), F32) for _ in range(HG_HEADS)]
            for j in range(SUB):
                g_j = gb_ref[slot, u * BLK + j * SUB:u * BLK + (j + 1) * SUB, :]
                slab = q[rs] * jnp.exp2(cumh[rs] - g_j)
                for hh in range(HG_HEADS):
                    pair_sum = jnp.sum(slab[:, hh * HG_DIM:(hh + 1) * HG_DIM], axis=-1, keepdims=True)
                    pieces[hh] = jnp.where(piece_col == u * SUB + j, pair_sum, pieces[hh])
            diag_pieces.append(pieces)
            if u % 2:
                yield
        yield from wait_for("i")
        v16 = proj_ref[rows, C_I:C_I + HG_WIDTH].astype(BF16)
        decay_rows = jnp.exp2(lasth)
        heads = [slice(hh * HG_DIM, (hh + 1) * HG_DIM) for hh in range(HG_HEADS)]
        lhs, rhs, decay_cols = [], [], []
        for hh, hs in enumerate(heads):
            att_d = jnp.concatenate([diag_pieces[u][hh] for u in range(BLK // SUB)], axis=0)
            a_h = (att_off[hh] + att_d).astype(BF16)
            lhs.append(jnp.concatenate([q_in[:, hs], a_h], axis=1))
            rhs.append(jnp.concatenate([ohg_ref[0, b, hh].astype(BF16), v16[:, hs]], axis=0))
            decay_cols.append(jnp.broadcast_to(decay_rows[:, hs], (HG_DIM, HG_DIM)).T)
        yield from _stages(PREP_LEAD)
        o_parts = [_dot(lhs[hh], rhs[hh]) for hh in range(HG_HEADS)]
        kv_new = [_dot_tn(k_out[:, hs], v16[:, hs]) for hs in heads]
        yield
        for hh in range(HG_HEADS):
            ohg_ref[0, b, hh] = ohg_ref[0, b, hh] * decay_cols[hh] + kv_new[hh]
        yield from wait_for("g")
        o_parts = [o_h * lax.rsqrt(jnp.mean(o_h * o_h, axis=-1, keepdims=True) + EPS) for o_h in o_parts]
        o = jnp.concatenate(o_parts, axis=1) * hgnw_ref[...]
        o = o * _silu(proj_ref[rows, C_G:C_G + HG_WIDTH])
        mix_ref[rows, SSM_WIDTH:SSM_WIDTH + HG_WIDTH] = o.astype(BF16)

    tasks = {}
    for part in range(N_PARTS):
        first = part * STREAMS_PER_PART
        streams = ["s%d" % b for b in range(first, first + STREAMS_PER_PART)]
        tasks["in%d" % part] = (in_projection(part), [] if part == 0 else ["in%d" % (part - 1)])
        for b in range(first, first + STREAMS_PER_PART):
            tasks["s%d" % b] = (per_stream(b, b % SLAB_SLOTS), [])
        tasks["out%d" % part] = (out_projection(part), streams)
    done, active, rnd, next_stream_start = set(), [], 0, 0
    pending = list(tasks)
    while pending or active:
        for name in list(pending):
            is_stream = name.startswith("s")
            if all(dep in done for dep in tasks[name][1]) and not (is_stream and rnd < next_stream_start):
                pending.remove(name)
                active.append(name)
                if is_stream:
                    next_stream_start = rnd + STAGGER[int(name[1:]) // STREAMS_PER_PART]
        for name in list(active):
            if next(tasks[name][0], True) is not None:
                active.remove(name)
                done.add(name)
        rnd += 1


def _row_broadcast_matrix():
    r = np.arange(SUB * BLK)
    u, j, i = r // BLK, (r % BLK) // SUB, r % SUB
    c = np.arange(4 * BLK)
    part, t = c // BLK, c % BLK
    pick = (part[None, :] < 3) & (t[None, :] == (u * SUB + j)[:, None])
    mask = (part[None, :] == 3) & (t[None, :] == 0) & (i < j)[:, None]
    return jnp.asarray(pick | mask, dtype=BF16)


def _mixer(h, p, layer, n_prompt_blocks, sconv, sssm, shg):
    n_steps = h.shape[0] // ROWS
    row_spec = pl.BlockSpec((ROWS, D_MODEL), lambda i: (i, 0))
    group = lambda i: i // n_prompt_blocks
    conv_shape = (2, N_STREAMS, SUB, CONV_CH)
    ssm_shape = (2, N_STREAMS, SSM_GROUPS, SSM_STATE, SSM_GROUP_WIDTH)
    hg_shape = (2, N_STREAMS, HG_HEADS, HG_DIM, HG_DIM)
    in_specs = [
        row_spec,
        _const_spec((1, D_MODEL)),
        _layer_spec((D_MODEL, IN_MAIN), layer),
        _layer_spec((D_MODEL, CONV_CH), layer),
        _layer_spec((D_MODEL, SSM_WIDTH), layer),
        _const_spec((CONV_WIDTH, CONV_CH)),
        _const_spec((1, CONV_CH)),
        _const_spec((1, SSM_WIDTH)),
        _const_spec((1, SSM_WIDTH)),
        _const_spec((1, SSM_WIDTH)),
        _const_spec((1, SSM_WIDTH)),
        _const_spec((p["lb_raw"].shape[0], HG_WIDTH)),
        _const_spec((1, HG_WIDTH)),
        _layer_spec((D_MODEL, D_MODEL), layer),
        _const_spec((SUB * BLK, 4 * BLK)),
        _layer_spec(conv_shape[1:], layer),
        _layer_spec(ssm_shape[1:], layer),
        _layer_spec(hg_shape[1:], layer),
    ]
    out_specs = [
        row_spec,
        pl.BlockSpec((1,) + conv_shape[1:], lambda i: (group(i), 0, 0, 0)),
        pl.BlockSpec((1,) + ssm_shape[1:], lambda i: (group(i), 0, 0, 0, 0)),
        pl.BlockSpec((1,) + hg_shape[1:], lambda i: (group(i), 0, 0, 0, 0)),
    ]
    out_shape = [
        jax.ShapeDtypeStruct(h.shape, F32),
        jax.ShapeDtypeStruct(conv_shape, F32),
        jax.ShapeDtypeStruct(ssm_shape, F32),
        jax.ShapeDtypeStruct(hg_shape, F32),
    ]
    return pl.pallas_call(
        functools.partial(_mixer_kernel, layer=layer, n_prompt_blocks=n_prompt_blocks),
        grid=(n_steps,),
        in_specs=in_specs,
        out_specs=out_specs,
        out_shape=out_shape,
        scratch_shapes=[
            pltpu.VMEM((ROWS, IN_MAIN), F32),
            pltpu.VMEM((N_STREAMS, CONV_TILES, SUB + BLK, LANES), F32),
            pltpu.VMEM((ROWS, SSM_WIDTH), F32),
            pltpu.VMEM((ROWS, D_MODEL), BF16),
            pltpu.VMEM((SLAB_SLOTS, SUB * BLK, HG_WIDTH), F32),
        ],
        compiler_params=pltpu.CompilerParams(
            dimension_semantics=("arbitrary",), vmem_limit_bytes=VMEM_LIMIT_BYTES),
        name="mixer",
    )(h, p["ln_mix"], p["w_in"], p["w_x"], p["w_dt"], p["conv_w"], p["conv_b"], p["dt_bias"], p["a_log"],
      p["d_skip"], p["ssm_norm"], p["lb_raw"], p["hg_norm"], p["w_out"],
      _row_broadcast_matrix(), sconv, sssm, shg)


def _expand_heads(x):
    return jnp.repeat(x.astype(F32), SSM_HEADDIM)[None, :]


def kernel(x_prompt, x_sample, state_conv, state_ssm, state_hgrn, meta_tokens, ln_ffa_w, ffa_w_gate, ffa_w_up, ffa_w_down, ln_mix_w, w_in, conv_w, conv_b, dt_bias, a_log, d_skip, ssm_norm_w, hg_lb_raw, hg_norm_w, w_out, ln_ffb_w, ffb_w_gate, ffb_w_up, ffb_w_down, ln_f_w):
    depth = w_in.shape[0]
    nb, seq, _ = x_prompt.shape
    assert nb == N_STREAMS and x_sample.shape[:2] == (N_STREAMS, BLK) and seq % BLK == 0
    n_seq_blocks = seq // BLK
    n_prompt_blocks = n_seq_blocks + 1

    row = lambda a: a.astype(F32)[None, :]
    dt_cols = slice(SSM_WIDTH + CONV_CH, SSM_WIDTH + CONV_CH + SSM_WIDTH // SSM_HEADDIM)
    w_main16 = jnp.concatenate([w_in[:, :, :SSM_WIDTH], w_in[:, :, dt_cols.stop:]], axis=2).astype(BF16)
    w_x16 = w_in[:, :, SSM_WIDTH:dt_cols.start].astype(BF16)
    w_dt16 = jnp.repeat(w_in[:, :, dt_cols], SSM_HEADDIM, axis=2).astype(BF16)
    w_out16 = w_out.astype(BF16)
    ffa16 = (ffa_w_gate.astype(F32), ffa_w_up.astype(F32), ffa_w_down.astype(F32))
    ffb16 = (ffb_w_gate.astype(F32), ffb_w_up.astype(F32), ffb_w_down.astype(F32))
    sconv_all = jnp.pad(state_conv.astype(F32), ((0, 0), (0, 0), (SUB - (CONV_WIDTH - 1), 0), (0, 0)))
    sssm_all = (state_ssm.astype(F32)
                .reshape(depth, nb, SSM_GROUPS, SSM_GROUP_WIDTH // SSM_HEADDIM, SSM_HEADDIM, SSM_STATE)
                .transpose(0, 1, 2, 5, 3, 4).reshape(depth, nb, SSM_GROUPS, SSM_STATE, SSM_GROUP_WIDTH))
    shg_all = state_hgrn.astype(F32)

    conv_out, ssm_out, hg_out = [], [], []
    for l in range(depth):
        params = {
            "ln_mix": row(ln_mix_w[l]),
            "w_in": w_main16,
            "w_x": w_x16,
            "w_dt": w_dt16,
            "conv_w": conv_w[l].astype(F32),
            "conv_b": row(conv_b[l]),
            "dt_bias": _expand_heads(dt_bias[l]),
            "a_log": _expand_heads(a_log[l]),
            "d_skip": _expand_heads(d_skip[l]),
            "ssm_norm": row(ssm_norm_w[l]),
            "lb_raw": hg_lb_raw.astype(F32),
            "hg_norm": row(hg_norm_w[l]),
            "w_out": w_out16,
        }
        if l == 0:
            h = _ffn_first(x_prompt.astype(F32), x_sample.astype(F32), meta_tokens.astype(F32),
                           l, row(ln_ffa_w[l]), *ffa16)
        else:
            h = _ffn(h, l, row(ln_ffa_w[l]), *ffa16)
        h, oc, os_, og = _mixer(h, params, l, n_prompt_blocks, sconv_all, sssm_all, shg_all)
        if l == depth - 1:
            y_prompt, y_sample = _ffn_last(h, l, row(ln_ffb_w[l]), *ffb16, row(ln_f_w))
        else:
            h = _ffn(h, l, row(ln_ffb_w[l]), *ffb16)
        conv_out.append(oc)
        ssm_out.append(os_)
        hg_out.append(og)

    conv_all = jnp.stack(conv_out)[:, :, :, SUB - (CONV_WIDTH - 1):, :]
    ssm_all = (jnp.stack(ssm_out)
               .reshape(depth, 2, nb, SSM_GROUPS, SSM_STATE, SSM_GROUP_WIDTH // SSM_HEADDIM, SSM_HEADDIM)
               .transpose(0, 1, 2, 3, 5, 6, 4)
               .reshape(depth, 2, nb, SSM_WIDTH // SSM_HEADDIM, SSM_HEADDIM, SSM_STATE))
    hg_all = jnp.stack(hg_out)
    dt_ = x_prompt.dtype
    return (y_prompt.astype(dt_), y_sample.astype(x_sample.dtype),
            conv_all[:, 0].astype(dt_), ssm_all[:, 0].astype(dt_), hg_all[:, 0].astype(dt_),
            conv_all[:, 1].astype(state_conv.dtype), ssm_all[:, 1].astype(state_ssm.dtype),
            hg_all[:, 1].astype(state_hgrn.dtype))
```

```python
import functools

import numpy as np
import jax
import jax.numpy as jnp
from jax import lax
from jax.experimental import pallas as pl
from jax.experimental.pallas import tpu as pltpu

F32 = jnp.float32
BF16 = jnp.bfloat16

D_MODEL = 1024
N_STREAMS = 8
BLK = 64
ROWS = N_STREAMS * BLK
N_META = 16
N_PAD = BLK - N_META
SSM_WIDTH = 512
SSM_GROUPS = 2
SSM_GROUP_WIDTH = SSM_WIDTH // SSM_GROUPS
SSM_HEADDIM = 64
SSM_STATE = 128
CONV_CH = 1024
CONV_WIDTH = 4
HG_WIDTH = 512
HG_HEADS = 4
HG_DIM = 128
D_FF = 2816
MXU_TILE = 256
FF_CHUNK_EDGES = (0, 6 * MXU_TILE, D_FF)
EPS = 1e-6
LB_FLOOR = 1e-30
SUB = 8
LEVELS = (32, 16, 8)
LOG2E = 1.4426950408889634

N_PARTS = 2
STREAMS_PER_PART = N_STREAMS // N_PARTS
PART_ROWS = STREAMS_PER_PART * BLK
PROJ_CHUNK = 256
STREAM_LAG = 1
PREP_LEAD = 2
LANES = 128
CONV_TILES = CONV_CH // LANES
BIG = float(2 ** 100)

C_Z, C_Q, C_F, C_I, C_G = 0, 512, 1024, 1536, 2048
IN_MAIN = 2560

VMEM_LIMIT_BYTES = 56 * 1024 * 1024


def _rms(x, w):
    return x * lax.rsqrt(jnp.mean(x * x, axis=-1, keepdims=True) + EPS) * w


def _silu(x):
    return x / (1.0 + jnp.exp(-x))


def _log1p(y):
    u = 1.0 + y
    return jnp.where(u == 1.0, y, jnp.log(u) * (y / (u - 1.0)))


def _softplus(x):
    return jnp.maximum(x, 0.0) + _log1p(jnp.exp(-jnp.abs(x)))


def _dot(a, b):
    return jnp.dot(a, b, preferred_element_type=F32)


def _dot_nt(a, b):
    return lax.dot_general(a, b, (((1,), (1,)), ((), ())), preferred_element_type=F32)


def _dot_tn(a, b):
    return lax.dot_general(a, b, (((0,), (0,)), ((), ())), preferred_element_type=F32)


def _embed_rows(piece, r0):
    s, w = piece.shape
    tile = 2 * SUB
    if s % tile:
        assert s == SUB and r0 % SUB == 0
        zeros = jnp.zeros((SUB, w), F32)
        piece = jnp.concatenate([zeros, piece] if r0 % tile else [piece, zeros], axis=0)
        r0, s = r0 - r0 % tile, tile
    assert r0 % tile == 0 and s % tile == 0
    parts = [piece.astype(BF16)]
    if r0:
        parts.insert(0, jnp.zeros((r0, w), BF16))
    if BLK - r0 - s:
        parts.append(jnp.zeros((BLK - r0 - s, w), BF16))
    return jnp.concatenate(parts, axis=0) if len(parts) > 1 else parts[0]


def _stages(n):
    for _ in range(n):
        yield


def _split3(x):
    hi = x.astype(BF16)
    r1 = x - hi.astype(F32)
    mid = r1.astype(BF16)
    lo = (r1 - mid.astype(F32)).astype(BF16)
    return jnp.concatenate([hi, mid, lo], axis=0)


def _swiglu_step(h, lnw_ref, wg_ref, wu_ref, wd_ref):
    hn = _rms(h, lnw_ref[...]).astype(BF16)
    acc = jnp.zeros_like(h)
    for c0, c1 in zip(FF_CHUNK_EDGES[:-1], FF_CHUNK_EDGES[1:]):
        cs = slice(c0, c1)
        g = _dot(hn, wg_ref[:, cs])
        u = _dot(hn, wu_ref[:, cs])
        a = (_silu(g) * u).astype(BF16)
        acc = acc + _dot(a, wd_ref[cs, :])
    return h + 0.5 * acc


def _fetch_weight_bf16(w_hbm, layer, w16_ref, stage_ref, sem):
    rows = stage_ref.shape[1]
    n_chunks = w16_ref.shape[0] // rows

    def chunk_copy(k):
        return pltpu.make_async_copy(w_hbm.at[layer, pl.ds(k * rows, rows), :],
                                     stage_ref.at[k % 2], sem.at[k % 2])

    chunk_copy(0).start()
    for k in range(n_chunks):
        if k + 1 < n_chunks:
            chunk_copy(k + 1).start()
        chunk_copy(k).wait()
        w16_ref[k * rows:(k + 1) * rows, :] = stage_ref[k % 2].astype(BF16)


def _fetch_ffn_weights(layer, wg_hbm, wu_hbm, wd_hbm, wg_ref, wu_ref, wd_ref, wide_stage, wide_sem,
                       narrow_stage, narrow_sem):
    @pl.when(pl.program_id(0) == 0)
    def _():
        _fetch_weight_bf16(wg_hbm, layer, wg_ref, wide_stage, wide_sem)
        _fetch_weight_bf16(wu_hbm, layer, wu_ref, wide_stage, wide_sem)
        _fetch_weight_bf16(wd_hbm, layer, wd_ref, narrow_stage, narrow_sem)


def _ffn_kernel(h_ref, lnw_ref, wg_hbm, wu_hbm, wd_hbm, o_ref, wg_ref, wu_ref, wd_ref, *stage, layer):
    _fetch_ffn_weights(layer, wg_hbm, wu_hbm, wd_hbm, wg_ref, wu_ref, wd_ref, *stage)
    o_ref[...] = _swiglu_step(h_ref[...], lnw_ref, wg_ref, wu_ref, wd_ref)


def _ffn_first_kernel(xp_ref, xs_ref, meta_ref, lnw_ref, wg_hbm, wu_hbm, wd_hbm, o_ref,
                      wg_ref, wu_ref, wd_ref, *stage, layer):
    _fetch_ffn_weights(layer, wg_hbm, wu_hbm, wd_hbm, wg_ref, wu_ref, wd_ref, *stage)
    step = pl.program_id(0)
    first = jnp.concatenate([jnp.zeros((N_PAD, D_MODEL), F32), meta_ref[...]], axis=0)
    h = jnp.where(step == 0, first[None], xp_ref[...])
    h = jnp.where(step == pl.num_programs(0) - 1, xs_ref[...], h)
    o_ref[...] = _swiglu_step(h.reshape(ROWS, D_MODEL), lnw_ref, wg_ref, wu_ref, wd_ref)


def _ffn_last_kernel(h_ref, lnw_ref, wg_hbm, wu_hbm, wd_hbm, lnf_ref, yp_ref, ys_ref,
                     wg_ref, wu_ref, wd_ref, *stage, layer):
    _fetch_ffn_weights(layer, wg_hbm, wu_hbm, wd_hbm, wg_ref, wu_ref, wd_ref, *stage)
    step = pl.program_id(0)
    last = pl.num_programs(0) - 1
    y = _rms(_swiglu_step(h_ref[...], lnw_ref, wg_ref, wu_ref, wd_ref), lnf_ref[...])
    y = y.reshape(N_STREAMS, BLK, D_MODEL)

    @pl.when(step < last)
    def _():
        yp_ref[...] = y

    @pl.when(step == last)
    def _():
        ys_ref[...] = y


def _const_spec(shape):
    nd = len(shape)
    return pl.BlockSpec(shape, lambda i: (0,) * nd, pipeline_mode=pl.Buffered(1))


def _layer_spec(shape, layer):
    nd = len(shape)
    return pl.BlockSpec((None,) + shape, lambda i: (layer,) + (0,) * nd, pipeline_mode=pl.Buffered(1))


_HBM_SPEC = pl.BlockSpec(memory_space=pl.ANY)
_FFN_WEIGHT_SPECS = [_const_spec((1, D_MODEL)), _HBM_SPEC, _HBM_SPEC, _HBM_SPEC]
FETCH_ROWS = MXU_TILE
_FFN_SCRATCH = [
    pltpu.VMEM((D_MODEL, D_FF), BF16), pltpu.VMEM((D_MODEL, D_FF), BF16), pltpu.VMEM((D_FF, D_MODEL), BF16),
    pltpu.VMEM((2, FETCH_ROWS, D_FF), F32), pltpu.SemaphoreType.DMA((2,)),
    pltpu.VMEM((2, FETCH_ROWS, D_MODEL), F32), pltpu.SemaphoreType.DMA((2,)),
]
_ROW_SPEC = pl.BlockSpec((ROWS, D_MODEL), lambda i: (i, 0))
_FFN_PARAMS = pltpu.CompilerParams(dimension_semantics=("arbitrary",), vmem_limit_bytes=VMEM_LIMIT_BYTES)


def _prompt_block_spec(n_seq_blocks):
    return pl.BlockSpec((N_STREAMS, BLK, D_MODEL),
                        lambda i: (0, jnp.clip(i - 1, 0, n_seq_blocks - 1), 0))


def _ffn(h, layer, lnw, wg, wu, wd):
    return pl.pallas_call(
        functools.partial(_ffn_kernel, layer=layer),
        grid=(h.shape[0] // ROWS,),
        in_specs=[_ROW_SPEC] + _FFN_WEIGHT_SPECS,
        out_specs=_ROW_SPEC,
        out_shape=jax.ShapeDtypeStruct(h.shape, F32),
        scratch_shapes=_FFN_SCRATCH,
        compiler_params=_FFN_PARAMS,
        name="ffn",
    )(h, lnw, wg, wu, wd)


def _ffn_first(x_prompt, x_sample, meta, layer, lnw, wg, wu, wd):
    n_seq_blocks = x_prompt.shape[1] // BLK
    n_steps = n_seq_blocks + 2
    return pl.pallas_call(
        functools.partial(_ffn_first_kernel, layer=layer),
        grid=(n_steps,),
        in_specs=[_prompt_block_spec(n_seq_blocks), _const_spec((N_STREAMS, BLK, D_MODEL)),
                  _const_spec((N_META, D_MODEL))] + _FFN_WEIGHT_SPECS,
        out_specs=_ROW_SPEC,
        out_shape=jax.ShapeDtypeStruct((n_steps * ROWS, D_MODEL), F32),
        scratch_shapes=_FFN_SCRATCH,
        compiler_params=_FFN_PARAMS,
        name="ffn_first",
    )(x_prompt, x_sample, meta, lnw, wg, wu, wd)


def _ffn_last(h, layer, lnw, wg, wu, wd, lnf):
    n_steps = h.shape[0] // ROWS
    n_seq_blocks = n_steps - 2
    return pl.pallas_call(
        functools.partial(_ffn_last_kernel, layer=layer),
        grid=(n_steps,),
        in_specs=[_ROW_SPEC] + _FFN_WEIGHT_SPECS + [_const_spec((1, D_MODEL))],
        out_specs=[_prompt_block_spec(n_seq_blocks),
                   pl.BlockSpec((N_STREAMS, BLK, D_MODEL), lambda i: (0, 0, 0))],
        out_shape=[jax.ShapeDtypeStruct((N_STREAMS, n_seq_blocks * BLK, D_MODEL), F32),
                   jax.ShapeDtypeStruct((N_STREAMS, BLK, D_MODEL), F32)],
        scratch_shapes=_FFN_SCRATCH,
        compiler_params=_FFN_PARAMS,
        name="ffn_last",
    )(h, lnw, wg, wu, wd, lnf)


def _mixer_kernel(h_ref, lnw_ref, win_ref, wx_ref, wdt_ref, convw_ref, convb_ref, dtb_ref, alog_ref,
                  dskip_ref, ssmnw_ref, lbraw_ref, hgnw_ref, wout_ref,
                  sconv_ref, sssm_ref, shg_ref,
                  hout_ref, oconv_ref, ossm_ref, ohg_ref,
                  proj_ref, xbc_ref, dtraw_ref, mix_ref, *, layer, n_prompt_blocks):
    step = pl.program_id(0)

    @pl.when(step == 0)
    def _():
        oconv_ref[...] = jnp.zeros(oconv_ref.shape, F32)
        ossm_ref[...] = jnp.zeros(ossm_ref.shape, F32)
        ohg_ref[...] = jnp.zeros(ohg_ref.shape, F32)

    @pl.when(step == n_prompt_blocks)
    def _():
        oconv_ref[0] = sconv_ref[...]
        ossm_ref[0] = sssm_ref[...]
        ohg_ref[0] = shg_ref[...]

    ready = [set() for _ in range(N_PARTS)]

    def in_projection(part):
        r0 = part * PART_ROWS
        rs = slice(r0, r0 + PART_ROWS)
        hn = _rms(h_ref[rs, :], lnw_ref[...]).astype(BF16)
        yield
        for c0 in range(0, CONV_CH, PROJ_CHUNK):
            x_new = _dot(hn, wx_ref[:, c0:c0 + PROJ_CHUNK])
            for k in range(STREAMS_PER_PART):
                for lt in range(PROJ_CHUNK // LANES):
                    xbc_ref[part * STREAMS_PER_PART + k, c0 // LANES + lt, SUB:SUB + BLK, :] = (
                        x_new[k * BLK:(k + 1) * BLK, lt * LANES:(lt + 1) * LANES])
            yield
        ready[part].add("x")
        for c0 in range(0, SSM_WIDTH, PROJ_CHUNK):
            dtraw_ref[rs, c0:c0 + PROJ_CHUNK] = _dot(hn, wdt_ref[:, c0:c0 + PROJ_CHUNK])
            yield
        ready[part].add("dt")
        for name, c_lo in (("z", C_Z), ("q", C_Q), ("f", C_F), ("i", C_I), ("g", C_G)):
            for c0 in range(c_lo, c_lo + HG_WIDTH, PROJ_CHUNK):
                proj_ref[rs, c0:c0 + PROJ_CHUNK] = _dot(hn, win_ref[:, c0:c0 + PROJ_CHUNK])
                yield
            ready[part].add(name)

    def out_projection(part):
        r0 = part * PART_ROWS
        rs = slice(r0, r0 + PART_ROWS)
        for c0 in range(0, D_MODEL, PROJ_CHUNK):
            cs = slice(c0, c0 + PROJ_CHUNK)
            hout_ref[rs, cs] = h_ref[rs, cs] + _dot(mix_ref[rs, :], wout_ref[:, cs])
            yield

    first_valid = jnp.where(step == 0, N_PAD, 0)
    t_col = lax.broadcasted_iota(jnp.int32, (BLK, 1), 0)
    valid = t_col >= first_valid

    tri3 = (lax.broadcasted_iota(jnp.int32, (BLK, 3 * BLK), 0)
            >= lax.broadcasted_iota(jnp.int32, (BLK, 3 * BLK), 1) % BLK).astype(BF16)

    conv_w = convw_ref[...]
    conv_b = convb_ref[...]
    a2 = -jnp.exp(alog_ref[...]) * LOG2E
    row_w = lax.broadcasted_iota(jnp.int32, (BLK, SSM_WIDTH), 0)
    col_w = lax.broadcasted_iota(jnp.int32, (BLK, SSM_WIDTH), 1)
    diag_pick = (col_w % BLK) == row_w
    row_g = lax.broadcasted_iota(jnp.int32, (BLK, SSM_GROUP_WIDTH), 0)
    col_g = lax.broadcasted_iota(jnp.int32, (BLK, SSM_GROUP_WIDTH), 1)
    causal_g = (col_g % BLK) <= row_g
    bd_r = lax.broadcasted_iota(jnp.int32, (SSM_GROUP_WIDTH, SSM_GROUP_WIDTH), 0)
    bd_c = lax.broadcasted_iota(jnp.int32, (SSM_GROUP_WIDTH, SSM_GROUP_WIDTH), 1)
    head_diag = ((bd_r // BLK) == (bd_c // BLK)).astype(BF16)

    raw = lbraw_ref[...]
    e_raw = jnp.exp(raw - jnp.max(raw, axis=0, keepdims=True))
    sm = e_raw / jnp.sum(e_raw, axis=0, keepdims=True)
    lb = jnp.clip(jnp.sum(sm[0:layer + 1], axis=0, keepdims=True) - sm[0:1], 0.0, 1.0 - 1e-6)
    log_lb = jnp.log(jnp.maximum(lb, LB_FLOOR))
    log_1m_lb = jnp.log1p(-lb)
    piece_col = lax.broadcasted_iota(jnp.int32, (SUB, BLK), 1)
    sub_rows = lax.broadcasted_iota(jnp.int32, (SUB, 1), 0)

    def per_stream(b):
        r0 = b * BLK if isinstance(b, int) else pl.multiple_of(b * BLK, BLK)
        rows = pl.ds(r0, BLK)
        projected = ready[b // STREAMS_PER_PART]

        def wait_for(name):
            while name not in projected:
                yield

        yield from wait_for("x")
        yield from _stages(STREAM_LAG * (b % STREAMS_PER_PART))
        tail = oconv_ref[0, b]
        conv_parts = []
        for lt in range(CONV_TILES):
            ls = slice(lt * LANES, (lt + 1) * LANES)
            xbc_ref[b, lt, 0:SUB, :] = tail[:, ls]
            part = conv_b[:, ls]
            for s in range(CONV_WIDTH - 1, -1, -1):
                tap = xbc_ref[b, lt, pl.ds(SUB - s, BLK, stride=1), :]
                part = part + tap * conv_w[CONV_WIDTH - 1 - s:CONV_WIDTH - s, ls]
            conv_parts.append(part)
        oconv_ref[0, b] = jnp.concatenate(
            [xbc_ref[b, lt, BLK:BLK + SUB, :] for lt in range(CONV_TILES)], axis=1)
        xbc = _silu(jnp.concatenate(conv_parts, axis=1))
        xs = xbc[:, 0:SSM_WIDTH]
        yield

        yield from wait_for("dt")
        dt = _softplus(dtraw_ref[rows, :] + dtb_ref[...])
        dt = jnp.where(valid, dt, 0.0)
        decay_pieces = _split3(dt * a2)
        groups = [slice(g * SSM_GROUP_WIDTH, (g + 1) * SSM_GROUP_WIDTH) for g in range(SSM_GROUPS)]
        b16 = [xbc[:, SSM_WIDTH + g * SSM_STATE:SSM_WIDTH + (g + 1) * SSM_STATE].astype(BF16)
               for g in range(SSM_GROUPS)]
        c16 = [xbc[:, SSM_WIDTH + (SSM_GROUPS + g) * SSM_STATE:
                   SSM_WIDTH + (SSM_GROUPS + g + 1) * SSM_STATE].astype(BF16) for g in range(SSM_GROUPS)]
        xdt = [xs[:, cs] * dt[:, cs] for cs in groups]
        xdt_bd = [jnp.concatenate([x.astype(BF16)] * 4, axis=0) * head_diag for x in xdt]
        st16 = [ossm_ref[0, b, g].astype(BF16) for g in range(SSM_GROUPS)]
        yield from _stages(PREP_LEAD)
        cum = _dot(tri3, decay_pieces)
        cbm = [_dot_nt(c16[g], jnp.concatenate([b16[g]] * 4, axis=0)) for g in range(SSM_GROUPS)]
        y_state = [_dot(c16[g], st16[g]) for g in range(SSM_GROUPS)]
        yield
        cum_row = jnp.sum(jnp.where(diag_pick, cum, 0.0), axis=0, keepdims=True)
        att16, xdt_w, last = [], [], []
        for g, cs in enumerate(groups):
            decay = jnp.where(causal_g, jnp.exp2(cum[:, cs] - cum_row[:, cs]), 0.0)
            att16.append((cbm[g] * decay).astype(BF16))
            last.append(cum[BLK - 1:BLK, cs])
            xdt_w.append((xdt[g] * jnp.exp2(last[g] - cum[:, cs])).astype(BF16))
        yield from _stages(PREP_LEAD)
        y_intra = [_dot(att16[g], xdt_bd[g]) for g in range(SSM_GROUPS)]
        st_new = [_dot_tn(b16[g], xdt_w[g]) for g in range(SSM_GROUPS)]
        yield
        yield from wait_for("z")
        y_parts = []
        for g, cs in enumerate(groups):
            ossm_ref[0, b, g] = jnp.exp2(last[g]) * ossm_ref[0, b, g] + st_new[g]
            y_g = y_intra[g] + y_state[g] * jnp.exp2(cum[:, cs]) + dskip_ref[:, cs] * xs[:, cs]
            y_g = y_g * _silu(proj_ref[rows, C_Z + g * SSM_GROUP_WIDTH:C_Z + (g + 1) * SSM_GROUP_WIDTH])
            y_g = y_g * lax.rsqrt(jnp.mean(y_g * y_g, axis=-1, keepdims=True) + EPS)
            y_parts.append(y_g * ssmnw_ref[:, cs])
        mix_ref[rows, 0:SSM_WIDTH] = jnp.concatenate(y_parts, axis=1).astype(BF16)
        yield

        yield from wait_for("f")
        q = _silu(proj_ref[rows, C_Q:C_Q + HG_WIDTH])
        fr = proj_ref[rows, C_F:C_F + HG_WIDTH]
        log_sig = jnp.minimum(fr, 0.0) - jnp.log(1.0 + jnp.exp(-jnp.abs(fr)))
        t2 = log_1m_lb + log_sig
        lf = jnp.maximum(log_lb, t2) + jnp.log(1.0 + jnp.exp(-jnp.abs(log_lb - t2)))
        lf = jnp.where(valid, lf, 0.0)
        log2_key = jnp.where(valid, (t2 - fr) * LOG2E, -BIG)
        forget_pieces = _split3(lf * LOG2E)
        yield from _stages(PREP_LEAD)
        cumh = _dot(tri3, forget_pieces)
        yield
        g = cumh - log2_key
        lasth = cumh[BLK - 1:BLK]
        q_in = (q * jnp.exp2(cumh)).astype(BF16)
        k_out = jnp.exp2(lasth - g).astype(BF16)

        q_cols, k_cols = [], []
        for s in LEVELS:
            for m in range(0, BLK, 2 * s):
                lo, up = slice(m, m + s), slice(m + s, m + 2 * s)
                mid = cumh[m + s - 1:m + s]
                k_cols.append(_embed_rows(jnp.exp2(mid - g[lo]), m))
                q_cols.append(_embed_rows(q[up] * jnp.exp2(cumh[up] - mid), m + s))
        yield from _stages(PREP_LEAD)
        att_off = []
        for hh in range(HG_HEADS):
            hs = slice(hh * HG_DIM, (hh + 1) * HG_DIM)
            att_off.append(_dot_nt(jnp.concatenate([c[:, hs] for c in q_cols], axis=1),
                                   jnp.concatenate([c[:, hs] for c in k_cols], axis=1)))
        yield
        diag_pieces = []
        for u in range(BLK // SUB):
            rs = slice(u * SUB, (u + 1) * SUB)
            pieces = [jnp.zeros((SUB, BLK), F32) for _ in range(HG_HEADS)]
            for j in range(SUB):
                g_j = jnp.broadcast_to(g[u * SUB + j:u * SUB + j + 1], (SUB, HG_WIDTH))
                slab = jnp.where(sub_rows >= j, q[rs] * jnp.exp2(cumh[rs] - g_j), 0.0)
                for hh in range(HG_HEADS):
                    pair_sum = jnp.sum(slab[:, hh * HG_DIM:(hh + 1) * HG_DIM], axis=-1, keepdims=True)
                    pieces[hh] = jnp.where(piece_col == u * SUB + j, pair_sum, pieces[hh])
            diag_pieces.append(pieces)
            if u % 2:
                yield
        yield from wait_for("i")
        v16 = proj_ref[rows, C_I:C_I + HG_WIDTH].astype(BF16)
        decay_rows = jnp.exp2(lasth)
        heads = [slice(hh * HG_DIM, (hh + 1) * HG_DIM) for hh in range(HG_HEADS)]
        lhs, rhs, decay_cols = [], [], []
        for hh, hs in enumerate(heads):
            att_d = jnp.concatenate([diag_pieces[u][hh] for u in range(BLK // SUB)], axis=0)
            a_h = (att_off[hh] + att_d).astype(BF16)
            lhs.append(jnp.concatenate([q_in[:, hs], a_h], axis=1))
            rhs.append(jnp.concatenate([ohg_ref[0, b, hh].astype(BF16), v16[:, hs]], axis=0))
            decay_cols.append(jnp.broadcast_to(decay_rows[:, hs], (HG_DIM, HG_DIM)).T)
        yield from _stages(PREP_LEAD)
        o_parts = [_dot(lhs[hh], rhs[hh]) for hh in range(HG_HEADS)]
        kv_new = [_dot_tn(k_out[:, hs], v16[:, hs]) for hs in heads]
        yield
        for hh in range(HG_HEADS):
            ohg_ref[0, b, hh] = ohg_ref[0, b, hh] * decay_cols[hh] + kv_new[hh]
        yield from wait_for("g")
        o_parts = [o_h * lax.rsqrt(jnp.mean(o_h * o_h, axis=-1, keepdims=True) + EPS) for o_h in o_parts]
        o = jnp.concatenate(o_parts, axis=1) * hgnw_ref[...]
        o = o * _silu(proj_ref[rows, C_G:C_G + HG_WIDTH])
        mix_ref[rows, SSM_WIDTH:SSM_WIDTH + HG_WIDTH] = o.astype(BF16)

    tasks = {}
    for part in range(N_PARTS):
        first = part * STREAMS_PER_PART
        streams = ["s%d" % b for b in range(first, first + STREAMS_PER_PART)]
        tasks["in%d" % part] = (in_projection(part), [] if part == 0 else ["in%d" % (part - 1)])
        for b in range(first, first + STREAMS_PER_PART):
            tasks["s%d" % b] = (per_stream(b), [])
        tasks["out%d" % part] = (out_projection(part), streams)
    done, active = set(), []
    pending = list(tasks)
    while pending or active:
        for name in list(pending):
            if all(dep in done for dep in tasks[name][1]):
                pending.remove(name)
                active.append(name)
        for name in list(active):
            if next(tasks[name][0], True) is not None:
                active.remove(name)
                done.add(name)


def _mixer(h, p, layer, n_prompt_blocks, sconv, sssm, shg):
    n_steps = h.shape[0] // ROWS
    row_spec = pl.BlockSpec((ROWS, D_MODEL), lambda i: (i, 0))
    group = lambda i: i // n_prompt_blocks
    conv_shape = (2, N_STREAMS, SUB, CONV_CH)
    ssm_shape = (2, N_STREAMS, SSM_GROUPS, SSM_STATE, SSM_GROUP_WIDTH)
    hg_shape = (2, N_STREAMS, HG_HEADS, HG_DIM, HG_DIM)
    in_specs = [
        row_spec,
        _const_spec((1, D_MODEL)),
        _layer_spec((D_MODEL, IN_MAIN), layer),
        _layer_spec((D_MODEL, CONV_CH), layer),
        _layer_spec((D_MODEL, SSM_WIDTH), layer),
        _const_spec((CONV_WIDTH, CONV_CH)),
        _const_spec((1, CONV_CH)),
        _const_spec((1, SSM_WIDTH)),
        _const_spec((1, SSM_WIDTH)),
        _const_spec((1, SSM_WIDTH)),
        _const_spec((1, SSM_WIDTH)),
        _const_spec((p["lb_raw"].shape[0], HG_WIDTH)),
        _const_spec((1, HG_WIDTH)),
        _layer_spec((D_MODEL, D_MODEL), layer),
        _layer_spec(conv_shape[1:], layer),
        _layer_spec(ssm_shape[1:], layer),
        _layer_spec(hg_shape[1:], layer),
    ]
    out_specs = [
        row_spec,
        pl.BlockSpec((1,) + conv_shape[1:], lambda i: (group(i), 0, 0, 0)),
        pl.BlockSpec((1,) + ssm_shape[1:], lambda i: (group(i), 0, 0, 0, 0)),
        pl.BlockSpec((1,) + hg_shape[1:], lambda i: (group(i), 0, 0, 0, 0)),
    ]
    out_shape = [
        jax.ShapeDtypeStruct(h.shape, F32),
        jax.ShapeDtypeStruct(conv_shape, F32),
        jax.ShapeDtypeStruct(ssm_shape, F32),
        jax.ShapeDtypeStruct(hg_shape, F32),
    ]
    return pl.pallas_call(
        functools.partial(_mixer_kernel, layer=layer, n_prompt_blocks=n_prompt_blocks),
        grid=(n_steps,),
        in_specs=in_specs,
        out_specs=out_specs,
        out_shape=out_shape,
        scratch_shapes=[
            pltpu.VMEM((ROWS, IN_MAIN), F32),
            pltpu.VMEM((N_STREAMS, CONV_TILES, SUB + BLK, LANES), F32),
            pltpu.VMEM((ROWS, SSM_WIDTH), F32),
            pltpu.VMEM((ROWS, D_MODEL), BF16),
        ],
        compiler_params=pltpu.CompilerParams(
            dimension_semantics=("arbitrary",), vmem_limit_bytes=VMEM_LIMIT_BYTES),
        name="mixer",
    )(h, p["ln_mix"], p["w_in"], p["w_x"], p["w_dt"], p["conv_w"], p["conv_b"], p["dt_bias"], p["a_log"],
      p["d_skip"], p["ssm_norm"], p["lb_raw"], p["hg_norm"], p["w_out"],
      sconv, sssm, shg)


def _expand_heads(x):
    return jnp.repeat(x.astype(F32), SSM_HEADDIM)[None, :]


def kernel(x_prompt, x_sample, state_conv, state_ssm, state_hgrn, meta_tokens, ln_ffa_w, ffa_w_gate, ffa_w_up, ffa_w_down, ln_mix_w, w_in, conv_w, conv_b, dt_bias, a_log, d_skip, ssm_norm_w, hg_lb_raw, hg_norm_w, w_out, ln_ffb_w, ffb_w_gate, ffb_w_up, ffb_w_down, ln_f_w):
    depth = w_in.shape[0]
    nb, seq, _ = x_prompt.shape
    assert nb == N_STREAMS and x_sample.shape[:2] == (N_STREAMS, BLK) and seq % BLK == 0
    n_seq_blocks = seq // BLK
    n_prompt_blocks = n_seq_blocks + 1

    row = lambda a: a.astype(F32)[None, :]
    dt_cols = slice(SSM_WIDTH + CONV_CH, SSM_WIDTH + CONV_CH + SSM_WIDTH // SSM_HEADDIM)
    w_main16 = jnp.concatenate([w_in[:, :, :SSM_WIDTH], w_in[:, :, dt_cols.stop:]], axis=2).astype(BF16)
    w_x16 = w_in[:, :, SSM_WIDTH:dt_cols.start].astype(BF16)
    w_dt16 = jnp.repeat(w_in[:, :, dt_cols], SSM_HEADDIM, axis=2).astype(BF16)
    w_out16 = w_out.astype(BF16)
    ffa16 = (ffa_w_gate.astype(F32), ffa_w_up.astype(F32), ffa_w_down.astype(F32))
    ffb16 = (ffb_w_gate.astype(F32), ffb_w_up.astype(F32), ffb_w_down.astype(F32))
    sconv_all = jnp.pad(state_conv.astype(F32), ((0, 0), (0, 0), (SUB - (CONV_WIDTH - 1), 0), (0, 0)))
    sssm_all = (state_ssm.astype(F32)
                .reshape(depth, nb, SSM_GROUPS, SSM_GROUP_WIDTH // SSM_HEADDIM, SSM_HEADDIM, SSM_STATE)
                .transpose(0, 1, 2, 5, 3, 4).reshape(depth, nb, SSM_GROUPS, SSM_STATE, SSM_GROUP_WIDTH))
    shg_all = state_hgrn.astype(F32)

    conv_out, ssm_out, hg_out = [], [], []
    for l in range(depth):
        params = {
            "ln_mix": row(ln_mix_w[l]),
            "w_in": w_main16,
            "w_x": w_x16,
            "w_dt": w_dt16,
            "conv_w": conv_w[l].astype(F32),
            "conv_b": row(conv_b[l]),
            "dt_bias": _expand_heads(dt_bias[l]),
            "a_log": _expand_heads(a_log[l]),
            "d_skip": _expand_heads(d_skip[l]),
            "ssm_norm": row(ssm_norm_w[l]),
            "lb_raw": hg_lb_raw.astype(F32),
            "hg_norm": row(hg_norm_w[l]),
            "w_out": w_out16,
        }
        if l == 0:
            h = _ffn_first(x_prompt.astype(F32), x_sample.astype(F32), meta_tokens.astype(F32),
                           l, row(ln_ffa_w[l]), *ffa16)
        else:
            h = _ffn(h, l, row(ln_ffa_w[l]), *ffa16)
        h, oc, os_, og = _mixer(h, params, l, n_prompt_blocks, sconv_all, sssm_all, shg_all)
        if l == depth - 1:
            y_prompt, y_sample = _ffn_last(h, l, row(ln_ffb_w[l]), *ffb16, row(ln_f_w))
        else:
            h = _ffn(h, l, row(ln_ffb_w[l]), *ffb16)
        conv_out.append(oc)
        ssm_out.append(os_)
        hg_out.append(og)

    conv_all = jnp.stack(conv_out)[:, :, :, SUB - (CONV_WIDTH - 1):, :]
    ssm_all = (jnp.stack(ssm_out)
               .reshape(depth, 2, nb, SSM_GROUPS, SSM_STATE, SSM_GROUP_WIDTH // SSM_HEADDIM, SSM_HEADDIM)
               .transpose(0, 1, 2, 3, 5, 6, 4)
               .reshape(depth, 2, nb, SSM_WIDTH // SSM_HEADDIM, SSM_HEADDIM, SSM_STATE))
    hg_all = jnp.stack(hg_out)
    dt_ = x_prompt.dtype
    return (y_prompt.astype(dt_), y_sample.astype(x_sample.dtype),
            conv_all[:, 0].astype(dt_), ssm_all[:, 0].astype(dt_), hg_all[:, 0].astype(dt_),
            conv_all[:, 1].astype(state_conv.dtype), ssm_all[:, 1].astype(state_ssm.dtype),
            hg_all[:, 1].astype(state_hgrn.dtype))
```

```python
import functools

import numpy as np
import jax
import jax.numpy as jnp
from jax import lax
from jax.experimental import pallas as pl
from jax.experimental.pallas import tpu as pltpu

F32 = jnp.float32
BF16 = jnp.bfloat16

D_MODEL = 1024
N_STREAMS = 8
BLK = 64
ROWS = N_STREAMS * BLK
N_META = 16
N_PAD = BLK - N_META
SSM_WIDTH = 512
SSM_GROUPS = 2
SSM_GROUP_WIDTH = SSM_WIDTH // SSM_GROUPS
SSM_HEADDIM = 64
SSM_STATE = 128
CONV_CH = 1024
CONV_WIDTH = 4
HG_WIDTH = 512
HG_HEADS = 4
HG_DIM = 128
D_FF = 2816
MXU_TILE = 256
FF_CHUNK_EDGES = (0, 6 * MXU_TILE, D_FF)
EPS = 1e-6
LB_FLOOR = 1e-30
SUB = 8
LEVELS = (32, 16, 8)
LOG2E = 1.4426950408889634

N_PARTS = 2
STREAMS_PER_PART = N_STREAMS // N_PARTS
PART_ROWS = STREAMS_PER_PART * BLK
PROJ_CHUNK = 256
STREAM_LAG = 1
PREP_LEAD = 1
LANES = 128
CONV_TILES = CONV_CH // LANES
BIG = float(2 ** 100)

C_Z, C_Q, C_F, C_I, C_G = 0, 512, 1024, 1536, 2048
IN_MAIN = 2560

VMEM_LIMIT_BYTES = 56 * 1024 * 1024


def _rms(x, w):
    return x * lax.rsqrt(jnp.mean(x * x, axis=-1, keepdims=True) + EPS) * w


def _silu(x):
    return x / (1.0 + jnp.exp(-x))


def _log1p(y):
    u = 1.0 + y
    return jnp.where(u == 1.0, y, jnp.log(u) * (y / (u - 1.0)))


def _softplus(x):
    return jnp.maximum(x, 0.0) + _log1p(jnp.exp(-jnp.abs(x)))


def _dot(a, b):
    return jnp.dot(a, b, preferred_element_type=F32)


def _dot_nt(a, b):
    return lax.dot_general(a, b, (((1,), (1,)), ((), ())), preferred_element_type=F32)


def _dot_tn(a, b):
    return lax.dot_general(a, b, (((0,), (0,)), ((), ())), preferred_element_type=F32)


def _embed_rows(piece, r0):
    s, w = piece.shape
    tile = 2 * SUB
    if s % tile:
        assert s == SUB and r0 % SUB == 0
        zeros = jnp.zeros((SUB, w), F32)
        piece = jnp.concatenate([zeros, piece] if r0 % tile else [piece, zeros], axis=0)
        r0, s = r0 - r0 % tile, tile
    assert r0 % tile == 0 and s % tile == 0
    parts = [piece.astype(BF16)]
    if r0:
        parts.insert(0, jnp.zeros((r0, w), BF16))
    if BLK - r0 - s:
        parts.append(jnp.zeros((BLK - r0 - s, w), BF16))
    return jnp.concatenate(parts, axis=0) if len(parts) > 1 else parts[0]


def _stages(n):
    for _ in range(n):
        yield


def _split3(x):
    hi = x.astype(BF16)
    r1 = x - hi.astype(F32)
    mid = r1.astype(BF16)
    lo = (r1 - mid.astype(F32)).astype(BF16)
    return jnp.concatenate([hi, mid, lo], axis=0)


def _swiglu_step(h, lnw_ref, wg_ref, wu_ref, wd_ref):
    hn = _rms(h, lnw_ref[...]).astype(BF16)
    acc = jnp.zeros_like(h)
    for c0, c1 in zip(FF_CHUNK_EDGES[:-1], FF_CHUNK_EDGES[1:]):
        cs = slice(c0, c1)
        g = _dot(hn, wg_ref[:, cs])
        u = _dot(hn, wu_ref[:, cs])
        a = (_silu(g) * u).astype(BF16)
        acc = acc + _dot(a, wd_ref[cs, :])
    return h + 0.5 * acc


def _fetch_weight_bf16(w_hbm, layer, w16_ref, stage_ref, sem):
    rows = stage_ref.shape[1]
    n_chunks = w16_ref.shape[0] // rows

    def chunk_copy(k):
        return pltpu.make_async_copy(w_hbm.at[layer, pl.ds(k * rows, rows), :],
                                     stage_ref.at[k % 2], sem.at[k % 2])

    chunk_copy(0).start()
    for k in range(n_chunks):
        if k + 1 < n_chunks:
            chunk_copy(k + 1).start()
        chunk_copy(k).wait()
        w16_ref[k * rows:(k + 1) * rows, :] = stage_ref[k % 2].astype(BF16)


def _fetch_ffn_weights(layer, wg_hbm, wu_hbm, wd_hbm, wg_ref, wu_ref, wd_ref, wide_stage, wide_sem,
                       narrow_stage, narrow_sem):
    @pl.when(pl.program_id(0) == 0)
    def _():
        _fetch_weight_bf16(wg_hbm, layer, wg_ref, wide_stage, wide_sem)
        _fetch_weight_bf16(wu_hbm, layer, wu_ref, wide_stage, wide_sem)
        _fetch_weight_bf16(wd_hbm, layer, wd_ref, narrow_stage, narrow_sem)


def _ffn_kernel(h_ref, lnw_ref, wg_hbm, wu_hbm, wd_hbm, o_ref, wg_ref, wu_ref, wd_ref, *stage, layer):
    _fetch_ffn_weights(layer, wg_hbm, wu_hbm, wd_hbm, wg_ref, wu_ref, wd_ref, *stage)
    o_ref[...] = _swiglu_step(h_ref[...], lnw_ref, wg_ref, wu_ref, wd_ref)


def _ffn_first_kernel(xp_ref, xs_ref, meta_ref, lnw_ref, wg_hbm, wu_hbm, wd_hbm, o_ref,
                      wg_ref, wu_ref, wd_ref, *stage, layer):
    _fetch_ffn_weights(layer, wg_hbm, wu_hbm, wd_hbm, wg_ref, wu_ref, wd_ref, *stage)
    step = pl.program_id(0)
    first = jnp.concatenate([jnp.zeros((N_PAD, D_MODEL), F32), meta_ref[...]], axis=0)
    h = jnp.where(step == 0, first[None], xp_ref[...])
    h = jnp.where(step == pl.num_programs(0) - 1, xs_ref[...], h)
    o_ref[...] = _swiglu_step(h.reshape(ROWS, D_MODEL), lnw_ref, wg_ref, wu_ref, wd_ref)


def _ffn_last_kernel(h_ref, lnw_ref, wg_hbm, wu_hbm, wd_hbm, lnf_ref, yp_ref, ys_ref,
                     wg_ref, wu_ref, wd_ref, *stage, layer):
    _fetch_ffn_weights(layer, wg_hbm, wu_hbm, wd_hbm, wg_ref, wu_ref, wd_ref, *stage)
    step = pl.program_id(0)
    last = pl.num_programs(0) - 1
    y = _rms(_swiglu_step(h_ref[...], lnw_ref, wg_ref, wu_ref, wd_ref), lnf_ref[...])
    y = y.reshape(N_STREAMS, BLK, D_MODEL)

    @pl.when(step < last)
    def _():
        yp_ref[...] = y

    @pl.when(step == last)
    def _():
        ys_ref[...] = y


def _const_spec(shape):
    nd = len(shape)
    return pl.BlockSpec(shape, lambda i: (0,) * nd, pipeline_mode=pl.Buffered(1))


def _layer_spec(shape, layer):
    nd = len(shape)
    return pl.BlockSpec((None,) + shape, lambda i: (layer,) + (0,) * nd, pipeline_mode=pl.Buffered(1))


_HBM_SPEC = pl.BlockSpec(memory_space=pl.ANY)
_FFN_WEIGHT_SPECS = [_const_spec((1, D_MODEL)), _HBM_SPEC, _HBM_SPEC, _HBM_SPEC]
FETCH_ROWS = MXU_TILE
_FFN_SCRATCH = [
    pltpu.VMEM((D_MODEL, D_FF), BF16), pltpu.VMEM((D_MODEL, D_FF), BF16), pltpu.VMEM((D_FF, D_MODEL), BF16),
    pltpu.VMEM((2, FETCH_ROWS, D_FF), F32), pltpu.SemaphoreType.DMA((2,)),
    pltpu.VMEM((2, FETCH_ROWS, D_MODEL), F32), pltpu.SemaphoreType.DMA((2,)),
]
_ROW_SPEC = pl.BlockSpec((ROWS, D_MODEL), lambda i: (i, 0))
_FFN_PARAMS = pltpu.CompilerParams(dimension_semantics=("arbitrary",), vmem_limit_bytes=VMEM_LIMIT_BYTES)


def _prompt_block_spec(n_seq_blocks):
    return pl.BlockSpec((N_STREAMS, BLK, D_MODEL),
                        lambda i: (0, jnp.clip(i - 1, 0, n_seq_blocks - 1), 0))


def _ffn(h, layer, lnw, wg, wu, wd):
    return pl.pallas_call(
        functools.partial(_ffn_kernel, layer=layer),
        grid=(h.shape[0] // ROWS,),
        in_specs=[_ROW_SPEC] + _FFN_WEIGHT_SPECS,
        out_specs=_ROW_SPEC,
        out_shape=jax.ShapeDtypeStruct(h.shape, F32),
        scratch_shapes=_FFN_SCRATCH,
        compiler_params=_FFN_PARAMS,
        name="ffn",
    )(h, lnw, wg, wu, wd)


def _ffn_first(x_prompt, x_sample, meta, layer, lnw, wg, wu, wd):
    n_seq_blocks = x_prompt.shape[1] // BLK
    n_steps = n_seq_blocks + 2
    return pl.pallas_call(
        functools.partial(_ffn_first_kernel, layer=layer),
        grid=(n_steps,),
        in_specs=[_prompt_block_spec(n_seq_blocks), _const_spec((N_STREAMS, BLK, D_MODEL)),
                  _const_spec((N_META, D_MODEL))] + _FFN_WEIGHT_SPECS,
        out_specs=_ROW_SPEC,
        out_shape=jax.ShapeDtypeStruct((n_steps * ROWS, D_MODEL), F32),
        scratch_shapes=_FFN_SCRATCH,
        compiler_params=_FFN_PARAMS,
        name="ffn_first",
    )(x_prompt, x_sample, meta, lnw, wg, wu, wd)


def _ffn_last(h, layer, lnw, wg, wu, wd, lnf):
    n_steps = h.shape[0] // ROWS
    n_seq_blocks = n_steps - 2
    return pl.pallas_call(
        functools.partial(_ffn_last_kernel, layer=layer),
        grid=(n_steps,),
        in_specs=[_ROW_SPEC] + _FFN_WEIGHT_SPECS + [_const_spec((1, D_MODEL))],
        out_specs=[_prompt_block_spec(n_seq_blocks),
                   pl.BlockSpec((N_STREAMS, BLK, D_MODEL), lambda i: (0, 0, 0))],
        out_shape=[jax.ShapeDtypeStruct((N_STREAMS, n_seq_blocks * BLK, D_MODEL), F32),
                   jax.ShapeDtypeStruct((N_STREAMS, BLK, D_MODEL), F32)],
        scratch_shapes=_FFN_SCRATCH,
        compiler_params=_FFN_PARAMS,
        name="ffn_last",
    )(h, lnw, wg, wu, wd, lnf)


def _mixer_kernel(h_ref, lnw_ref, win_ref, wx_ref, wdt_ref, convw_ref, convb_ref, dtb_ref, alog_ref,
                  dskip_ref, ssmnw_ref, lbraw_ref, hgnw_ref, wout_ref,
                  sconv_ref, sssm_ref, shg_ref,
                  hout_ref, oconv_ref, ossm_ref, ohg_ref,
                  proj_ref, xbc_ref, dtraw_ref, mix_ref, *, layer, n_prompt_blocks):
    step = pl.program_id(0)

    @pl.when(step == 0)
    def _():
        oconv_ref[...] = jnp.zeros(oconv_ref.shape, F32)
        ossm_ref[...] = jnp.zeros(ossm_ref.shape, F32)
        ohg_ref[...] = jnp.zeros(ohg_ref.shape, F32)

    @pl.when(step == n_prompt_blocks)
    def _():
        oconv_ref[0] = sconv_ref[...]
        ossm_ref[0] = sssm_ref[...]
        ohg_ref[0] = shg_ref[...]

    ready = [set() for _ in range(N_PARTS)]

    def in_projection(part):
        r0 = part * PART_ROWS
        rs = slice(r0, r0 + PART_ROWS)
        hn = _rms(h_ref[rs, :], lnw_ref[...]).astype(BF16)
        yield
        for c0 in range(0, CONV_CH, PROJ_CHUNK):
            x_new = _dot(hn, wx_ref[:, c0:c0 + PROJ_CHUNK])
            for k in range(STREAMS_PER_PART):
                for lt in range(PROJ_CHUNK // LANES):
                    xbc_ref[part * STREAMS_PER_PART + k, c0 // LANES + lt, SUB:SUB + BLK, :] = (
                        x_new[k * BLK:(k + 1) * BLK, lt * LANES:(lt + 1) * LANES])
            yield
        ready[part].add("x")
        for c0 in range(0, SSM_WIDTH, PROJ_CHUNK):
            dtraw_ref[rs, c0:c0 + PROJ_CHUNK] = _dot(hn, wdt_ref[:, c0:c0 + PROJ_CHUNK])
            yield
        ready[part].add("dt")
        for name, c_lo in (("z", C_Z), ("q", C_Q), ("f", C_F), ("i", C_I), ("g", C_G)):
            for c0 in range(c_lo, c_lo + HG_WIDTH, PROJ_CHUNK):
                proj_ref[rs, c0:c0 + PROJ_CHUNK] = _dot(hn, win_ref[:, c0:c0 + PROJ_CHUNK])
                yield
            ready[part].add(name)

    def out_projection(part):
        r0 = part * PART_ROWS
        rs = slice(r0, r0 + PART_ROWS)
        for c0 in range(0, D_MODEL, PROJ_CHUNK):
            cs = slice(c0, c0 + PROJ_CHUNK)
            hout_ref[rs, cs] = h_ref[rs, cs] + _dot(mix_ref[rs, :], wout_ref[:, cs])
            yield

    first_valid = jnp.where(step == 0, N_PAD, 0)
    t_col = lax.broadcasted_iota(jnp.int32, (BLK, 1), 0)
    valid = t_col >= first_valid

    tri3 = (lax.broadcasted_iota(jnp.int32, (BLK, 3 * BLK), 0)
            >= lax.broadcasted_iota(jnp.int32, (BLK, 3 * BLK), 1) % BLK).astype(BF16)

    conv_w = convw_ref[...]
    conv_b = convb_ref[...]
    a2 = -jnp.exp(alog_ref[...]) * LOG2E
    row_w = lax.broadcasted_iota(jnp.int32, (BLK, SSM_WIDTH), 0)
    col_w = lax.broadcasted_iota(jnp.int32, (BLK, SSM_WIDTH), 1)
    diag_pick = (col_w % BLK) == row_w
    row_g = lax.broadcasted_iota(jnp.int32, (BLK, SSM_GROUP_WIDTH), 0)
    col_g = lax.broadcasted_iota(jnp.int32, (BLK, SSM_GROUP_WIDTH), 1)
    causal_g = (col_g % BLK) <= row_g
    bd_r = lax.broadcasted_iota(jnp.int32, (SSM_GROUP_WIDTH, SSM_GROUP_WIDTH), 0)
    bd_c = lax.broadcasted_iota(jnp.int32, (SSM_GROUP_WIDTH, SSM_GROUP_WIDTH), 1)
    head_diag = ((bd_r // BLK) == (bd_c // BLK)).astype(BF16)

    raw = lbraw_ref[...]
    e_raw = jnp.exp(raw - jnp.max(raw, axis=0, keepdims=True))
    sm = e_raw / jnp.sum(e_raw, axis=0, keepdims=True)
    lb = jnp.clip(jnp.sum(sm[0:layer + 1], axis=0, keepdims=True) - sm[0:1], 0.0, 1.0 - 1e-6)
    log_lb = jnp.log(jnp.maximum(lb, LB_FLOOR))
    log_1m_lb = jnp.log1p(-lb)
    piece_col = lax.broadcasted_iota(jnp.int32, (SUB, BLK), 1)
    sub_rows = lax.broadcasted_iota(jnp.int32, (SUB, 1), 0)

    def per_stream(b):
        r0 = b * BLK if isinstance(b, int) else pl.multiple_of(b * BLK, BLK)
        rows = pl.ds(r0, BLK)
        projected = ready[b // STREAMS_PER_PART]

        def wait_for(name):
            while name not in projected:
                yield

        yield from wait_for("x")
        yield from _stages(STREAM_LAG * (b % STREAMS_PER_PART))
        tail = oconv_ref[0, b]
        conv_parts = []
        for lt in range(CONV_TILES):
            ls = slice(lt * LANES, (lt + 1) * LANES)
            xbc_ref[b, lt, 0:SUB, :] = tail[:, ls]
            part = conv_b[:, ls]
            for s in range(CONV_WIDTH - 1, -1, -1):
                tap = xbc_ref[b, lt, pl.ds(SUB - s, BLK, stride=1), :]
                part = part + tap * conv_w[CONV_WIDTH - 1 - s:CONV_WIDTH - s, ls]
            conv_parts.append(part)
        oconv_ref[0, b] = jnp.concatenate(
            [xbc_ref[b, lt, BLK:BLK + SUB, :] for lt in range(CONV_TILES)], axis=1)
        xbc = _silu(jnp.concatenate(conv_parts, axis=1))
        xs = xbc[:, 0:SSM_WIDTH]
        yield

        yield from wait_for("dt")
        dt = _softplus(dtraw_ref[rows, :] + dtb_ref[...])
        dt = jnp.where(valid, dt, 0.0)
        decay_pieces = _split3(dt * a2)
        groups = [slice(g * SSM_GROUP_WIDTH, (g + 1) * SSM_GROUP_WIDTH) for g in range(SSM_GROUPS)]
        b16 = [xbc[:, SSM_WIDTH + g * SSM_STATE:SSM_WIDTH + (g + 1) * SSM_STATE].astype(BF16)
               for g in range(SSM_GROUPS)]
        c16 = [xbc[:, SSM_WIDTH + (SSM_GROUPS + g) * SSM_STATE:
                   SSM_WIDTH + (SSM_GROUPS + g + 1) * SSM_STATE].astype(BF16) for g in range(SSM_GROUPS)]
        xdt = [xs[:, cs] * dt[:, cs] for cs in groups]
        xdt_bd = [jnp.concatenate([x.astype(BF16)] * 4, axis=0) * head_diag for x in xdt]
        st16 = [ossm_ref[0, b, g].astype(BF16) for g in range(SSM_GROUPS)]
        yield from _stages(PREP_LEAD)
        cum = _dot(tri3, decay_pieces)
        cbm = [_dot_nt(c16[g], jnp.concatenate([b16[g]] * 4, axis=0)) for g in range(SSM_GROUPS)]
        y_state = [_dot(c16[g], st16[g]) for g in range(SSM_GROUPS)]
        yield
        cum_row = jnp.sum(jnp.where(diag_pick, cum, 0.0), axis=0, keepdims=True)
        att16, xdt_w, last = [], [], []
        for g, cs in enumerate(groups):
            decay = jnp.where(causal_g, jnp.exp2(cum[:, cs] - cum_row[:, cs]), 0.0)
            att16.append((cbm[g] * decay).astype(BF16))
            last.append(cum[BLK - 1:BLK, cs])
            xdt_w.append((xdt[g] * jnp.exp2(last[g] - cum[:, cs])).astype(BF16))
        yield from _stages(PREP_LEAD)
        y_intra = [_dot(att16[g], xdt_bd[g]) for g in range(SSM_GROUPS)]
        st_new = [_dot_tn(b16[g], xdt_w[g]) for g in range(SSM_GROUPS)]
        yield
        yield from wait_for("z")
        y_parts = []
        for g, cs in enumerate(groups):
            ossm_ref[0, b, g] = jnp.exp2(last[g]) * ossm_ref[0, b, g] + st_new[g]
            y_g = y_intra[g] + y_state[g] * jnp.exp2(cum[:, cs]) + dskip_ref[:, cs] * xs[:, cs]
            y_g = y_g * _silu(proj_ref[rows, C_Z + g * SSM_GROUP_WIDTH:C_Z + (g + 1) * SSM_GROUP_WIDTH])
            y_g = y_g * lax.rsqrt(jnp.mean(y_g * y_g, axis=-1, keepdims=True) + EPS)
            y_parts.append(y_g * ssmnw_ref[:, cs])
        mix_ref[rows, 0:SSM_WIDTH] = jnp.concatenate(y_parts, axis=1).astype(BF16)
        yield

        yield from wait_for("f")
        q = _silu(proj_ref[rows, C_Q:C_Q + HG_WIDTH])
        fr = proj_ref[rows, C_F:C_F + HG_WIDTH]
        log_sig = jnp.minimum(fr, 0.0) - jnp.log(1.0 + jnp.exp(-jnp.abs(fr)))
        t2 = log_1m_lb + log_sig
        lf = jnp.maximum(log_lb, t2) + jnp.log(1.0 + jnp.exp(-jnp.abs(log_lb - t2)))
        lf = jnp.where(valid, lf, 0.0)
        log2_key = jnp.where(valid, (t2 - fr) * LOG2E, -BIG)
        forget_pieces = _split3(lf * LOG2E)
        yield from _stages(PREP_LEAD)
        cumh = _dot(tri3, forget_pieces)
        yield
        g = cumh - log2_key
        lasth = cumh[BLK - 1:BLK]
        q_in = (q * jnp.exp2(cumh)).astype(BF16)
        k_out = jnp.exp2(lasth - g).astype(BF16)

        q_cols, k_cols = [], []
        for s in LEVELS:
            for m in range(0, BLK, 2 * s):
                lo, up = slice(m, m + s), slice(m + s, m + 2 * s)
                mid = cumh[m + s - 1:m + s]
                k_cols.append(_embed_rows(jnp.exp2(mid - g[lo]), m))
                q_cols.append(_embed_rows(q[up] * jnp.exp2(cumh[up] - mid), m + s))
        yield from _stages(PREP_LEAD)
        att_off = []
        for hh in range(HG_HEADS):
            hs = slice(hh * HG_DIM, (hh + 1) * HG_DIM)
            att_off.append(_dot_nt(jnp.concatenate([c[:, hs] for c in q_cols], axis=1),
                                   jnp.concatenate([c[:, hs] for c in k_cols], axis=1)))
        yield
        diag_pieces = []
        for u in range(BLK // SUB):
            rs = slice(u * SUB, (u + 1) * SUB)
            pieces = [jnp.zeros((SUB, BLK), F32) for _ in range(HG_HEADS)]
            for j in range(SUB):
                g_j = jnp.broadcast_to(g[u * SUB + j:u * SUB + j + 1], (SUB, HG_WIDTH))
                slab = jnp.where(sub_rows >= j, q[rs] * jnp.exp2(cumh[rs] - g_j), 0.0)
                for hh in range(HG_HEADS):
                    pair_sum = jnp.sum(slab[:, hh * HG_DIM:(hh + 1) * HG_DIM], axis=-1, keepdims=True)
                    pieces[hh] = jnp.where(piece_col == u * SUB + j, pair_sum, pieces[hh])
            diag_pieces.append(pieces)
            if u % 2:
                yield
        yield from wait_for("i")
        v16 = proj_ref[rows, C_I:C_I + HG_WIDTH].astype(BF16)
        decay_rows = jnp.exp2(lasth)
        heads = [slice(hh * HG_DIM, (hh + 1) * HG_DIM) for hh in range(HG_HEADS)]
        lhs, rhs, decay_cols = [], [], []
        for hh, hs in enumerate(heads):
            att_d = jnp.concatenate([diag_pieces[u][hh] for u in range(BLK // SUB)], axis=0)
            a_h = (att_off[hh] + att_d).astype(BF16)
            lhs.append(jnp.concatenate([q_in[:, hs], a_h], axis=1))
            rhs.append(jnp.concatenate([ohg_ref[0, b, hh].astype(BF16), v16[:, hs]], axis=0))
            decay_cols.append(jnp.broadcast_to(decay_rows[:, hs], (HG_DIM, HG_DIM)).T)
        yield from _stages(PREP_LEAD)
        o_parts = [_dot(lhs[hh], rhs[hh]) for hh in range(HG_HEADS)]
        kv_new = [_dot_tn(k_out[:, hs], v16[:, hs]) for hs in heads]
        yield
        for hh in range(HG_HEADS):
            ohg_ref[0, b, hh] = ohg_ref[0, b, hh] * decay_cols[hh] + kv_new[hh]
        yield from wait_for("g")
        o_parts = [o_h * lax.rsqrt(jnp.mean(o_h * o_h, axis=-1, keepdims=True) + EPS) for o_h in o_parts]
        o = jnp.concatenate(o_parts, axis=1) * hgnw_ref[...]
        o = o * _silu(proj_ref[rows, C_G:C_G + HG_WIDTH])
        mix_ref[rows, SSM_WIDTH:SSM_WIDTH + HG_WIDTH] = o.astype(BF16)

    tasks = {}
    for part in range(N_PARTS):
        first = part * STREAMS_PER_PART
        streams = ["s%d" % b for b in range(first, first + STREAMS_PER_PART)]
        tasks["in%d" % part] = (in_projection(part), [] if part == 0 else ["in%d" % (part - 1)])
        for b in range(first, first + STREAMS_PER_PART):
            tasks["s%d" % b] = (per_stream(b), [])
        tasks["out%d" % part] = (out_projection(part), streams)
    done, active = set(), []
    pending = list(tasks)
    while pending or active:
        for name in list(pending):
            if all(dep in done for dep in tasks[name][1]):
                pending.remove(name)
                active.append(name)
        for name in list(active):
            if next(tasks[name][0], True) is not None:
                active.remove(name)
                done.add(name)


def _mixer(h, p, layer, n_prompt_blocks, sconv, sssm, shg):
    n_steps = h.shape[0] // ROWS
    row_spec = pl.BlockSpec((ROWS, D_MODEL), lambda i: (i, 0))
    group = lambda i: i // n_prompt_blocks
    conv_shape = (2, N_STREAMS, SUB, CONV_CH)
    ssm_shape = (2, N_STREAMS, SSM_GROUPS, SSM_STATE, SSM_GROUP_WIDTH)
    hg_shape = (2, N_STREAMS, HG_HEADS, HG_DIM, HG_DIM)
    in_specs = [
        row_spec,
        _const_spec((1, D_MODEL)),
        _layer_spec((D_MODEL, IN_MAIN), layer),
        _layer_spec((D_MODEL, CONV_CH), layer),
        _layer_spec((D_MODEL, SSM_WIDTH), layer),
        _const_spec((CONV_WIDTH, CONV_CH)),
        _const_spec((1, CONV_CH)),
        _const_spec((1, SSM_WIDTH)),
        _const_spec((1, SSM_WIDTH)),
        _const_spec((1, SSM_WIDTH)),
        _const_spec((1, SSM_WIDTH)),
        _const_spec((p["lb_raw"].shape[0], HG_WIDTH)),
        _const_spec((1, HG_WIDTH)),
        _layer_spec((D_MODEL, D_MODEL), layer),
        _layer_spec(conv_shape[1:], layer),
        _layer_spec(ssm_shape[1:], layer),
        _layer_spec(hg_shape[1:], layer),
    ]
    out_specs = [
        row_spec,
        pl.BlockSpec((1,) + conv_shape[1:], lambda i: (group(i), 0, 0, 0)),
        pl.BlockSpec((1,) + ssm_shape[1:], lambda i: (group(i), 0, 0, 0, 0)),
        pl.BlockSpec((1,) + hg_shape[1:], lambda i: (group(i), 0, 0, 0, 0)),
    ]
    out_shape = [
        jax.ShapeDtypeStruct(h.shape, F32),
        jax.ShapeDtypeStruct(conv_shape, F32),
        jax.ShapeDtypeStruct(ssm_shape, F32),
        jax.ShapeDtypeStruct(hg_shape, F32),
    ]
    return pl.pallas_call(
        functools.partial(_mixer_kernel, layer=layer, n_prompt_blocks=n_prompt_blocks),
        grid=(n_steps,),
        in_specs=in_specs,
        out_specs=out_specs,
        out_shape=out_shape,
        scratch_shapes=[
            pltpu.VMEM((ROWS, IN_MAIN), F32),
            pltpu.VMEM((N_STREAMS, CONV_TILES, SUB + BLK, LANES), F32),
            pltpu.VMEM((ROWS, SSM_WIDTH), F32),
            pltpu.VMEM((ROWS, D_MODEL), BF16),
        ],
        compiler_params=pltpu.CompilerParams(
            dimension_semantics=("arbitrary",), vmem_limit_bytes=VMEM_LIMIT_BYTES),
        name="mixer",
    )(h, p["ln_mix"], p["w_in"], p["w_x"], p["w_dt"], p["conv_w"], p["conv_b"], p["dt_bias"], p["a_log"],
      p["d_skip"], p["ssm_norm"], p["lb_raw"], p["hg_norm"], p["w_out"],
      sconv, sssm, shg)


def _expand_heads(x):
    return jnp.repeat(x.astype(F32), SSM_HEADDIM)[None, :]


def kernel(x_prompt, x_sample, state_conv, state_ssm, state_hgrn, meta_tokens, ln_ffa_w, ffa_w_gate, ffa_w_up, ffa_w_down, ln_mix_w, w_in, conv_w, conv_b, dt_bias, a_log, d_skip, ssm_norm_w, hg_lb_raw, hg_norm_w, w_out, ln_ffb_w, ffb_w_gate, ffb_w_up, ffb_w_down, ln_f_w):
    depth = w_in.shape[0]
    nb, seq, _ = x_prompt.shape
    assert nb == N_STREAMS and x_sample.shape[:2] == (N_STREAMS, BLK) and seq % BLK == 0
    n_seq_blocks = seq // BLK
    n_prompt_blocks = n_seq_blocks + 1

    row = lambda a: a.astype(F32)[None, :]
    dt_cols = slice(SSM_WIDTH + CONV_CH, SSM_WIDTH + CONV_CH + SSM_WIDTH // SSM_HEADDIM)
    w_main16 = jnp.concatenate([w_in[:, :, :SSM_WIDTH], w_in[:, :, dt_cols.stop:]], axis=2).astype(BF16)
    w_x16 = w_in[:, :, SSM_WIDTH:dt_cols.start].astype(BF16)
    w_dt16 = jnp.repeat(w_in[:, :, dt_cols], SSM_HEADDIM, axis=2).astype(BF16)
    w_out16 = w_out.astype(BF16)
    ffa16 = (ffa_w_gate.astype(F32), ffa_w_up.astype(F32), ffa_w_down.astype(F32))
    ffb16 = (ffb_w_gate.astype(F32), ffb_w_up.astype(F32), ffb_w_down.astype(F32))
    sconv_all = jnp.pad(state_conv.astype(F32), ((0, 0), (0, 0), (SUB - (CONV_WIDTH - 1), 0), (0, 0)))
    sssm_all = (state_ssm.astype(F32)
                .reshape(depth, nb, SSM_GROUPS, SSM_GROUP_WIDTH // SSM_HEADDIM, SSM_HEADDIM, SSM_STATE)
                .transpose(0, 1, 2, 5, 3, 4).reshape(depth, nb, SSM_GROUPS, SSM_STATE, SSM_GROUP_WIDTH))
    shg_all = state_hgrn.astype(F32)

    conv_out, ssm_out, hg_out = [], [], []
    for l in range(depth):
        params = {
            "ln_mix": row(ln_mix_w[l]),
            "w_in": w_main16,
            "w_x": w_x16,
            "w_dt": w_dt16,
            "conv_w": conv_w[l].astype(F32),
            "conv_b": row(conv_b[l]),
            "dt_bias": _expand_heads(dt_bias[l]),
            "a_log": _expand_heads(a_log[l]),
            "d_skip": _expand_heads(d_skip[l]),
            "ssm_norm": row(ssm_norm_w[l]),
            "lb_raw": hg_lb_raw.astype(F32),
            "hg_norm": row(hg_norm_w[l]),
            "w_out": w_out16,
        }
        if l == 0:
            h = _ffn_first(x_prompt.astype(F32), x_sample.astype(F32), meta_tokens.astype(F32),
                           l, row(ln_ffa_w[l]), *ffa16)
        else:
            h = _ffn(h, l, row(ln_ffa_w[l]), *ffa16)
        h, oc, os_, og = _mixer(h, params, l, n_prompt_blocks, sconv_all, sssm_all, shg_all)
        if l == depth - 1:
            y_prompt, y_sample = _ffn_last(h, l, row(ln_ffb_w[l]), *ffb16, row(ln_f_w))
        else:
            h = _ffn(h, l, row(ln_ffb_w[l]), *ffb16)
        conv_out.append(oc)
        ssm_out.append(os_)
        hg_out.append(og)

    conv_all = jnp.stack(conv_out)[:, :, :, SUB - (CONV_WIDTH - 1):, :]
    ssm_all = (jnp.stack(ssm_out)
               .reshape(depth, 2, nb, SSM_GROUPS, SSM_STATE, SSM_GROUP_WIDTH // SSM_HEADDIM, SSM_HEADDIM)
               .transpose(0, 1, 2, 3, 5, 6, 4)
               .reshape(depth, 2, nb, SSM_WIDTH // SSM_HEADDIM, SSM_HEADDIM, SSM_STATE))
    hg_all = jnp.stack(hg_out)
    dt_ = x_prompt.dtype
    return (y_prompt.astype(dt_), y_sample.astype(x_sample.dtype),
            conv_all[:, 0].astype(dt_), ssm_all[:, 0].astype(dt_), hg_all[:, 0].astype(dt_),
            conv_all[:, 1].astype(state_conv.dtype), ssm_all[:, 1].astype(state_ssm.dtype),
            hg_all[:, 1].astype(state_hgrn.dtype))
```
